```python
import math
import jax, jax.numpy as jnp
from jax import lax
import numpy as np

D_MODEL = 4096
BATCH = 1
SEQ = 8192
DEPTH = 1

N_META = 16
BLOCK = 128
PAD = (-N_META) % BLOCK
HEAD_DIM = 128
DIFF_V_DIM = 2 * HEAD_DIM
DIFF_WIDTH = D_MODEL // 2
FOX_WIDTH = D_MODEL - DIFF_WIDTH
N_DIFF_HEADS = DIFF_WIDTH // DIFF_V_DIM
N_FOX_HEADS = FOX_WIDTH // HEAD_DIM
MIX_WIDTH = DIFF_WIDTH + FOX_WIDTH
D_FF = 256 * ((8 * D_MODEL // 3 + 255) // 256)
SPLIT_SIZES = (DIFF_WIDTH, DIFF_WIDTH, DIFF_WIDTH, FOX_WIDTH, FOX_WIDTH, FOX_WIDTH, N_FOX_HEADS)
IN_COLS = sum(SPLIT_SIZES)
RMS_EPS = 1e-6
SUBLN_EPS = 1e-5
NEG_INF = -1e30

kernel_name = 'hymba_diff_fox_macaron_alibi'


def rms_norm(x, g, eps=RMS_EPS):
    xf = x.astype(jnp.float32)
    y = xf * lax.rsqrt(jnp.mean(xf * xf, axis=-1, keepdims=True) + eps)
    return (y * g.astype(jnp.float32)).astype(x.dtype)


def swiglu(h, w_gate, w_up, w_down):
    return (jax.nn.silu(h @ w_gate) * (h @ w_up)) @ w_down


def alibi_slopes(n):
    return 2.0 ** (-8.0 * jnp.arange(1, n + 1, dtype=jnp.float32) / n)


def hybrid_mixer(h, w_in, b_forget, lambda_q1, lambda_k1, lambda_q2, lambda_k2,
                 subln_gain, w_out, lambda_init):
    B, L, _ = h.shape
    proj = h @ w_in
    proj = jnp.pad(proj, ((0, 0), (PAD, 0), (0, 0)))
    Lp = L + PAD
    n_blocks = Lp // BLOCK
    q_d, k_d, v_d, q_f, k_f, v_f, f_logit = jnp.split(
        proj, np.cumsum(SPLIT_SIZES)[:-1].tolist(), axis=-1)
    q_d = q_d.reshape(B, Lp, N_DIFF_HEADS, 2, HEAD_DIM)
    k_d = k_d.reshape(B, Lp, N_DIFF_HEADS, 2, HEAD_DIM)
    v_d = v_d.reshape(B, Lp, N_DIFF_HEADS, DIFF_V_DIM)
    q_f = q_f.reshape(B, Lp, N_FOX_HEADS, HEAD_DIM)
    k_f = k_f.reshape(B, Lp, N_FOX_HEADS, HEAD_DIM)
    v_f = v_f.reshape(B, Lp, N_FOX_HEADS, HEAD_DIM)

    pos = jnp.arange(Lp)
    key_valid = pos >= PAD
    scale = HEAD_DIM ** -0.5

    log_f = jax.nn.log_sigmoid(f_logit.astype(jnp.float32) + b_forget.astype(jnp.float32))
    log_f = jnp.where(key_valid[None, :, None], log_f, 0.0)
    cum = jnp.swapaxes(jnp.cumsum(log_f, axis=1), 1, 2)

    lam = (jnp.exp(jnp.sum(lambda_q1.astype(jnp.float32) * lambda_k1.astype(jnp.float32)))
           - jnp.exp(jnp.sum(lambda_q2.astype(jnp.float32) * lambda_k2.astype(jnp.float32)))
           + lambda_init)
    slopes = alibi_slopes(N_DIFF_HEADS)

    def block(i):
        start = i * BLOCK
        qpos = start + jnp.arange(BLOCK)
        rel = (qpos[:, None] - pos[None, :]).astype(jnp.float32)
        mask = (rel >= 0) & key_valid[None, :]
        qd = lax.dynamic_slice_in_dim(q_d, start, BLOCK, axis=1)
        s = jnp.einsum('bqhcd,bkhcd->bhcqk', qd, k_d).astype(jnp.float32) * scale
        s = s - slopes[:, None, None, None] * rel
        p = jax.nn.softmax(jnp.where(mask, s, NEG_INF), axis=-1)
        a = (p[:, :, 0] - lam * p[:, :, 1]).astype(v_d.dtype)
        o_d = jnp.einsum('bhqk,bkhe->bqhe', a, v_d)
        qf = lax.dynamic_slice_in_dim(q_f, start, BLOCK, axis=1)
        cq = lax.dynamic_slice_in_dim(cum, start, BLOCK, axis=2)
        s = jnp.einsum('bqhd,bkhd->bhqk', qf, k_f).astype(jnp.float32) * scale
        s = s + cq[..., :, None] - cum[..., None, :]
        p = jax.nn.softmax(jnp.where(mask, s, NEG_INF), axis=-1)
        o_f = jnp.einsum('bhqk,bkhd->bqhd', p.astype(v_f.dtype), v_f)
        return o_d, o_f

    o_d, o_f = lax.map(block, jnp.arange(n_blocks))
    o_d = jnp.moveaxis(o_d, 0, 1).reshape(B, Lp, N_DIFF_HEADS, DIFF_V_DIM)[:, PAD:]
    o_f = jnp.moveaxis(o_f, 0, 1).reshape(B, Lp, N_FOX_HEADS, HEAD_DIM)[:, PAD:]
    o_d = rms_norm(o_d, subln_gain, SUBLN_EPS) * (1.0 - lambda_init)
    o = jnp.concatenate([o_d.reshape(B, L, DIFF_WIDTH), o_f.reshape(B, L, FOX_WIDTH)], axis=-1)
    return o @ w_out


def setup_inputs(seed: int = 0) -> dict:
    key = jax.random.key(seed)
    ks = jax.random.split(key, 20)
    f32 = jnp.float32
    n = lambda k, shape, s: jax.random.normal(k, shape, f32) * s
    gain = lambda k, shape: 1.0 + 0.02 * jax.random.normal(k, shape, f32)
    return {
        'x': jax.random.normal(ks[0], (BATCH, SEQ, D_MODEL), f32),
        'meta_tokens': n(ks[1], (N_META, D_MODEL), 1.0),
        'ffn1_norm': gain(ks[2], (DEPTH, D_MODEL)),
        'ffn1_w_gate': n(ks[3], (DEPTH, D_MODEL, D_FF), D_MODEL ** -0.5),
        'ffn1_w_up': n(ks[4], (DEPTH, D_MODEL, D_FF), D_MODEL ** -0.5),
        'ffn1_w_down': n(ks[5], (DEPTH, D_FF, D_MODEL), D_FF ** -0.5),
        'mix_norm': gain(ks[6], (DEPTH, D_MODEL)),
        'w_in': n(ks[7], (DEPTH, D_MODEL, IN_COLS), D_MODEL ** -0.5),
        'b_forget': n(ks[8], (DEPTH, N_FOX_HEADS), 0.5) + 1.0,
        'lambda_q1': n(ks[9], (DEPTH, HEAD_DIM), 0.1),
        'lambda_k1': n(ks[10], (DEPTH, HEAD_DIM), 0.1),
        'lambda_q2': n(ks[11], (DEPTH, HEAD_DIM), 0.1),
        'lambda_k2': n(ks[12], (DEPTH, HEAD_DIM), 0.1),
        'subln_gain': gain(ks[13], (DEPTH, DIFF_V_DIM)),
        'w_out': n(ks[14], (DEPTH, MIX_WIDTH, D_MODEL), MIX_WIDTH ** -0.5),
        'ffn2_norm': gain(ks[15], (DEPTH, D_MODEL)),
        'ffn2_w_gate': n(ks[16], (DEPTH, D_MODEL, D_FF), D_MODEL ** -0.5),
        'ffn2_w_up': n(ks[17], (DEPTH, D_MODEL, D_FF), D_MODEL ** -0.5),
        'ffn2_w_down': n(ks[18], (DEPTH, D_FF, D_MODEL), D_FF ** -0.5),
        'final_norm': gain(ks[19], (D_MODEL,)),
    }


def reference(x, meta_tokens, ffn1_norm, ffn1_w_gate, ffn1_w_up, ffn1_w_down,
              mix_norm, w_in, b_forget, lambda_q1, lambda_k1, lambda_q2, lambda_k2,
              subln_gain, w_out, ffn2_norm, ffn2_w_gate, ffn2_w_up, ffn2_w_down,
              final_norm):
    B = x.shape[0]
    meta = jnp.broadcast_to(meta_tokens[None].astype(x.dtype), (B, N_META, D_MODEL))
    h = jnp.concatenate([meta, x], axis=1)
    for l in range(DEPTH):
        lambda_init = 0.8 - 0.6 * math.exp(-0.3 * l)
        h = h + 0.5 * swiglu(rms_norm(h, ffn1_norm[l]), ffn1_w_gate[l], ffn1_w_up[l], ffn1_w_down[l])
        h = h + hybrid_mixer(rms_norm(h, mix_norm[l]), w_in[l], b_forget[l],
                             lambda_q1[l], lambda_k1[l], lambda_q2[l], lambda_k2[l],
                             subln_gain[l], w_out[l], lambda_init)
        h = h + 0.5 * swiglu(rms_norm(h, ffn2_norm[l]), ffn2_w_gate[l], ffn2_w_up[l], ffn2_w_down[l])
    return rms_norm(h, final_norm)[:, N_META:]
```

```python
import functools
import math

import jax
import jax.numpy as jnp
from jax import lax
from jax.experimental import pallas as pl
from jax.experimental.pallas import tpu as pltpu

N_META = 16
META_BLOCK = 128
HEAD_DIM = 128
DIFF_V_DIM = 2 * HEAD_DIM
RMS_EPS = 1e-6
SUBLN_EPS = 1e-5
NEG_INF = -1e30
LOG2E = math.log2(math.e)
VMEM_LIMIT_BYTES = 56 * 1024 * 1024

F32 = jnp.float32
BF16 = jnp.bfloat16


def _pick(n, target, mult):
    best = None
    for d in range(mult, min(n, target) + 1, mult):
        if n % d == 0:
            best = d
    assert best is not None, (n, target, mult)
    return best


def _params(*sem):
    return pltpu.CompilerParams(dimension_semantics=sem, vmem_limit_bytes=VMEM_LIMIT_BYTES)


def _rmsnorm_kernel(h_ref, g_ref, o_ref, *, eps):
    x = h_ref[...]
    ms = jnp.mean(x * x, axis=-1, keepdims=True)
    o_ref[...] = (x * lax.rsqrt(ms + eps) * g_ref[...]).astype(o_ref.dtype)


def _rmsnorm(h, g, rows, out_dtype):
    d = h.shape[1]
    tm = _pick(rows, 520, 8)
    return pl.pallas_call(
        functools.partial(_rmsnorm_kernel, eps=RMS_EPS),
        grid=(rows // tm,),
        in_specs=[pl.BlockSpec((tm, d), lambda i: (i, 0)),
                  pl.BlockSpec((1, d), lambda i: (0, 0))],
        out_specs=pl.BlockSpec((tm, d), lambda i: (i, 0)),
        out_shape=jax.ShapeDtypeStruct((rows, d), out_dtype),
        compiler_params=_params("arbitrary"),
        name="rmsnorm",
    )(h, g.reshape(1, d).astype(F32))


def _gate_up_kernel(n_ref, wg_ref, wu_ref, o_ref):
    n = n_ref[...]
    g = jnp.dot(n, wg_ref[...], preferred_element_type=F32)
    u = jnp.dot(n, wu_ref[...], preferred_element_type=F32)
    o_ref[...] = (g * jax.nn.sigmoid(g) * u).astype(o_ref.dtype)


def _gate_up(n, wg, wu):
    rows, d = n.shape
    f = wg.shape[1]
    tm = _pick(rows, 1040, 16)
    tf = _pick(f, 256, 128)
    return pl.pallas_call(
        _gate_up_kernel,
        grid=(rows // tm, f // tf),
        in_specs=[pl.BlockSpec((tm, d), lambda i, j: (i, 0)),
                  pl.BlockSpec((d, tf), lambda i, j: (0, j)),
                  pl.BlockSpec((d, tf), lambda i, j: (0, j))],
        out_specs=pl.BlockSpec((tm, tf), lambda i, j: (i, j)),
        out_shape=jax.ShapeDtypeStruct((rows, f), BF16),
        compiler_params=_params("arbitrary", "arbitrary"),
        name="ffn_gate_up",
    )(n, wg, wu)


def _down_kernel(a_ref, wd_ref, h_ref, o_ref):
    o_ref[...] = h_ref[...] + 0.5 * jnp.dot(a_ref[...], wd_ref[...], preferred_element_type=F32)


def _down_residual(act, wd, h):
    rows, f = act.shape
    d = wd.shape[1]
    tm = _pick(rows, 520, 16)
    tn = _pick(d, 512, 128)
    return pl.pallas_call(
        _down_kernel,
        grid=(rows // tm, d // tn),
        in_specs=[pl.BlockSpec((tm, f), lambda i, j: (i, 0)),
                  pl.BlockSpec((f, tn), lambda i, j: (0, j)),
                  pl.BlockSpec((tm, tn), lambda i, j: (i, j))],
        out_specs=pl.BlockSpec((tm, tn), lambda i, j: (i, j)),
        out_shape=jax.ShapeDtypeStruct((rows, d), F32),
        compiler_params=_params("arbitrary", "arbitrary"),
        name="ffn_down",
    )(act, wd, h)


def _swiglu_residual(h, rows, norm_g, wg, wu, wd):
    n = _rmsnorm(h, norm_g, rows, BF16)
    act = _gate_up(n, wg.astype(BF16), wu.astype(BF16))
    return _down_residual(act, wd.astype(BF16), h)


def _in_proj_kernel(n_ref, w_ref, s_ref, o_ref):
    y = jnp.dot(n_ref[...], w_ref[...], preferred_element_type=F32)
    o_ref[...] = (y * s_ref[...]).astype(o_ref.dtype)


def _in_proj(n, w, col_scale):
    rows, d = n.shape
    cols = w.shape[1]
    tm = _pick(rows, 1040, 16)
    tn = _pick(cols, 1024, 128)
    return pl.pallas_call(
        _in_proj_kernel,
        grid=(rows // tm, cols // tn),
        in_specs=[pl.BlockSpec((tm, d), lambda i, j: (i, 0)),
                  pl.BlockSpec((d, tn), lambda i, j: (0, j)),
                  pl.BlockSpec((1, tn), lambda i, j: (0, j))],
        out_specs=pl.BlockSpec((tm, tn), lambda i, j: (i, j)),
        out_shape=jax.ShapeDtypeStruct((rows, cols), BF16),
        compiler_params=_params("arbitrary", "arbitrary"),
        name="mixer_in_proj",
    )(n, w, col_scale)


def _forget_kernel(n_ref, wf_ref, b_ref, tri_ref, c_ref, carry_ref, *, n_blocks):
    blk = pl.program_id(0)

    @pl.when(blk == 0)
    def _():
        carry_ref[...] = jnp.zeros_like(carry_ref)

    logit = lax.dot_general(wf_ref[...], n_ref[...], (((1,), (1,)), ((), ())),
                            preferred_element_type=F32) + b_ref[...]
    log_f = jnp.minimum(logit, 0.0) - jnp.log1p(jnp.exp(-jnp.abs(logit)))
    is_meta = blk == n_blocks - 1
    col = lax.broadcasted_iota(jnp.int32, log_f.shape, 1)
    log_f = jnp.where(jnp.logical_and(is_meta, col >= N_META), 0.0, log_f)
    hi = log_f.astype(BF16)
    rest = log_f - hi.astype(F32)
    mid = rest.astype(BF16)
    lo = (rest - mid.astype(F32)).astype(BF16)
    tri = tri_ref[...]
    cs = (jnp.dot(hi, tri, preferred_element_type=F32)
          + jnp.dot(mid, tri, preferred_element_type=F32)
          + jnp.dot(lo, tri, preferred_element_type=F32))
    total = cs[:, META_BLOCK - 1:META_BLOCK]
    carry = carry_ref[...]
    c_ref[...] = jnp.where(is_meta, cs - total, cs + carry)
    carry_ref[...] = carry + total


def _forget_cumsum(n, wf_t, b_forget):
    rows, d = n.shape
    hf = wf_t.shape[0]
    n_blocks = rows // META_BLOCK
    tri = jnp.triu(jnp.ones((META_BLOCK, META_BLOCK), F32)).astype(BF16)
    return pl.pallas_call(
        functools.partial(_forget_kernel, n_blocks=n_blocks),
        grid=(n_blocks,),
        in_specs=[pl.BlockSpec((META_BLOCK, d), lambda b: (b, 0)),
                  pl.BlockSpec((hf, d), lambda b: (0, 0)),
                  pl.BlockSpec((hf, 1), lambda b: (0, 0)),
                  pl.BlockSpec((META_BLOCK, META_BLOCK), lambda b: (0, 0))],
        out_specs=pl.BlockSpec((hf, META_BLOCK), lambda b: (0, b)),
        out_shape=jax.ShapeDtypeStruct((hf, rows), F32),
        scratch_shapes=[pltpu.VMEM((hf, META_BLOCK), F32)],
        compiler_params=_params("arbitrary"),
        name="forget_cumsum",
    )(n, wf_t, b_forget.reshape(hf, 1).astype(F32), tri)


def _softmax_step(q, k_blk, v_blk, bias, mask, m_ref, l_ref, acc_ref):
    s = lax.dot_general(q, k_blk, (((1,), (1,)), ((), ())), preferred_element_type=F32)
    s = s + bias
    if mask is not None:
        s = jnp.where(mask, s, NEG_INF)
    m_prev = m_ref[...]
    m_new = jnp.maximum(m_prev, jnp.max(s, axis=1, keepdims=True))
    alpha = jnp.exp2(m_prev - m_new)
    p = jnp.exp2(s - m_new)
    l_ref[...] = alpha * l_ref[...] + jnp.sum(p, axis=1, keepdims=True)
    acc_ref[...] = alpha * acc_ref[...] + jnp.dot(p.astype(v_blk.dtype), v_blk,
                                                  preferred_element_type=F32)
    m_ref[...] = m_new


def _causal_sweep(step, qi, tq, seq):
    meta_mask = lax.broadcasted_iota(jnp.int32, (tq, META_BLOCK), 1) < N_META
    step(seq, META_BLOCK, meta_mask)

    def body(j, carry):
        step(pl.multiple_of(j * tq, tq), tq, None)
        return carry

    lax.fori_loop(0, qi, body, 0)
    diag_mask = (lax.broadcasted_iota(jnp.int32, (tq, tq), 0)
                 >= lax.broadcasted_iota(jnp.int32, (tq, tq), 1))
    step(pl.multiple_of(qi * tq, tq), tq, diag_mask)


def _fox_kernel(q_ref, k_ref, v_ref, c_ref, o_ref, m_ref, l_ref, acc_ref, *, tq, seq):
    qi = pl.program_id(1)
    q = q_ref[...]
    m_ref[...] = jnp.full_like(m_ref, NEG_INF)
    l_ref[...] = jnp.zeros_like(l_ref)
    acc_ref[...] = jnp.zeros_like(acc_ref)
    q0 = pl.multiple_of(qi * tq, tq)
    c_q = c_ref[0, :, pl.ds(q0, tq)][:, tq - 1:tq]

    def step(k0, width, mask):
        bias = (c_q - c_ref[0, :, pl.ds(k0, width)]) * LOG2E
        _softmax_step(q, k_ref[pl.ds(k0, width), :], v_ref[pl.ds(k0, width), :], bias, mask,
                      m_ref, l_ref, acc_ref)

    _causal_sweep(step, qi, tq, seq)
    o_ref[...] = (acc_ref[...] / l_ref[...]).astype(o_ref.dtype)


def _fox_attention(proj, c, seq, n_heads, q_col, k_col, v_col):
    rows = proj.shape[0]
    tq = _pick(seq, 512, 128)
    qb, kb, vb = q_col // HEAD_DIM, k_col // HEAD_DIM, v_col // HEAD_DIM
    return pl.pallas_call(
        functools.partial(_fox_kernel, tq=tq, seq=seq),
        grid=(n_heads, seq // tq),
        in_specs=[pl.BlockSpec((tq, HEAD_DIM), lambda h, i: (i, qb + h)),
                  pl.BlockSpec((rows, HEAD_DIM), lambda h, i: (0, kb + h)),
                  pl.BlockSpec((rows, HEAD_DIM), lambda h, i: (0, vb + h)),
                  pl.BlockSpec((1, 1, rows), lambda h, i: (h, 0, 0))],
        out_specs=pl.BlockSpec((tq, HEAD_DIM), lambda h, i: (i, h)),
        out_shape=jax.ShapeDtypeStruct((seq, n_heads * HEAD_DIM), BF16),
        scratch_shapes=[pltpu.VMEM((tq, 1), F32), pltpu.VMEM((tq, 1), F32),
                        pltpu.VMEM((tq, HEAD_DIM), F32)],
        compiler_params=_params("arbitrary", "arbitrary"),
        name="fox_attention",
    )(proj, proj, proj, c.reshape(n_heads, 1, rows))


def _diff_kernel(q_ref, k_ref, v_ref, lq1_ref, lk1_ref, lq2_ref, lk2_ref, gain_ref, o_ref,
                 m_ref, l_ref, acc_ref, *, tq, seq, n_heads, lambda_init):
    h = pl.program_id(0)
    qi = pl.program_id(1)
    m_ref[...] = jnp.full_like(m_ref, NEG_INF)
    l_ref[...] = jnp.zeros_like(l_ref)
    acc_ref[...] = jnp.zeros_like(acc_ref)
    slope = jnp.exp2(jnp.full((1, 1), -8.0 / n_heads, F32) * (h + 1).astype(F32)) * LOG2E
    q_last = qi * tq + (tq - 1)

    def step(k0, width, mask):
        key_off = jnp.where(k0 >= seq, -N_META - q_last, k0 - q_last)
        rel = (lax.broadcasted_iota(jnp.int32, (1, width), 1) + key_off).astype(F32)
        bias = slope * rel
        v_blk = v_ref[pl.ds(k0, width), :]
        for c in range(2):
            lanes = slice(c * HEAD_DIM, (c + 1) * HEAD_DIM)
            _softmax_step(q_ref[:, lanes], k_ref[pl.ds(k0, width), lanes], v_blk, bias, mask,
                          m_ref.at[c], l_ref.at[c], acc_ref.at[c])

    _causal_sweep(step, qi, tq, seq)
    lam = (jnp.exp(jnp.sum(lq1_ref[...] * lk1_ref[...], axis=1, keepdims=True))
           - jnp.exp(jnp.sum(lq2_ref[...] * lk2_ref[...], axis=1, keepdims=True))
           + lambda_init)
    o = acc_ref[0] / l_ref[0] - lam * (acc_ref[1] / l_ref[1])
    ms = jnp.mean(o * o, axis=-1, keepdims=True)
    o = o * lax.rsqrt(ms + SUBLN_EPS) * gain_ref[...]
    o_ref[...] = (o * (1.0 - lambda_init)).astype(o_ref.dtype)


def _diff_attention(proj, seq, n_heads, q_col, k_col, v_col, lam_vecs, gain, lambda_init):
    rows = proj.shape[0]
    tq = _pick(seq, 512, 128)
    qb, kb, vb = q_col // DIFF_V_DIM, k_col // DIFF_V_DIM, v_col // DIFF_V_DIM
    vec = pl.BlockSpec((1, HEAD_DIM), lambda h, i: (0, 0))
    return pl.pallas_call(
        functools.partial(_diff_kernel, tq=tq, seq=seq, n_heads=n_heads, lambda_init=lambda_init),
        grid=(n_heads, seq // tq),
        in_specs=[pl.BlockSpec((tq, DIFF_V_DIM), lambda h, i: (i, qb + h)),
                  pl.BlockSpec((rows, DIFF_V_DIM), lambda h, i: (0, kb + h)),
                  pl.BlockSpec((rows, DIFF_V_DIM), lambda h, i: (0, vb + h)),
                  vec, vec, vec, vec,
                  pl.BlockSpec((1, DIFF_V_DIM), lambda h, i: (0, 0))],
        out_specs=pl.BlockSpec((tq, DIFF_V_DIM), lambda h, i: (i, h)),
        out_shape=jax.ShapeDtypeStruct((seq, n_heads * DIFF_V_DIM), BF16),
        scratch_shapes=[pltpu.VMEM((2, tq, 1), F32), pltpu.VMEM((2, tq, 1), F32),
                        pltpu.VMEM((2, tq, DIFF_V_DIM), F32)],
        compiler_params=_params("arbitrary", "arbitrary"),
        name="diff_attention",
    )(proj, proj, proj, *[v.reshape(1, HEAD_DIM).astype(F32) for v in lam_vecs],
      gain.reshape(1, DIFF_V_DIM).astype(F32))


def _out_proj_kernel(od_ref, of_ref, wd_ref, wf_ref, h_ref, o_ref):
    o_ref[...] = (h_ref[...]
                  + jnp.dot(od_ref[...], wd_ref[...], preferred_element_type=F32)
                  + jnp.dot(of_ref[...], wf_ref[...], preferred_element_type=F32))


def _out_proj_residual(o_d, o_f, w_out, h):
    seq, dw = o_d.shape
    fw = o_f.shape[1]
    d = w_out.shape[1]
    tm = _pick(seq, 1024, 16)
    tn = _pick(d, 512, 128)
    assert dw % tn == 0 or dw == fw
    return pl.pallas_call(
        _out_proj_kernel,
        grid=(seq // tm, d // tn),
        in_specs=[pl.BlockSpec((tm, dw), lambda i, j: (i, 0)),
                  pl.BlockSpec((tm, fw), lambda i, j: (i, 0)),
                  pl.BlockSpec((dw, tn), lambda i, j: (0, j)),
                  pl.BlockSpec((fw, tn), lambda i, j: (dw // fw, j)),
                  pl.BlockSpec((tm, tn), lambda i, j: (i, j))],
        out_specs=pl.BlockSpec((tm, tn), lambda i, j: (i, j)),
        out_shape=jax.ShapeDtypeStruct((seq, d), F32),
        compiler_params=_params("arbitrary", "arbitrary"),
        name="mixer_out_proj",
    )(o_d, o_f, w_out, w_out, h)


def kernel(x, meta_tokens, ffn1_norm, ffn1_w_gate, ffn1_w_up, ffn1_w_down, mix_norm, w_in, b_forget, lambda_q1, lambda_k1, lambda_q2, lambda_k2, subln_gain, w_out, ffn2_norm, ffn2_w_gate, ffn2_w_up, ffn2_w_down, final_norm):
    batch, seq, d = x.shape
    depth = ffn1_norm.shape[0]
    assert batch == 1 and depth == 1 and meta_tokens.shape[0] == N_META
    diff_width = d // 2
    fox_width = d - diff_width
    n_diff = diff_width // DIFF_V_DIM
    n_fox = fox_width // HEAD_DIM
    qkv_cols = 3 * diff_width + 3 * fox_width
    lambda_init = 0.8 - 0.6 * math.exp(-0.3 * 0)

    rows = seq + META_BLOCK
    h = jnp.concatenate(
        [x[0], meta_tokens.astype(x.dtype), jnp.zeros((META_BLOCK - N_META, d), x.dtype)], axis=0)

    h = _swiglu_residual(h, rows, ffn1_norm[0], ffn1_w_gate[0], ffn1_w_up[0], ffn1_w_down[0])

    n = _rmsnorm(h, mix_norm[0], rows, BF16)
    q_scale = HEAD_DIM ** -0.5 * LOG2E
    col_scale = jnp.concatenate([
        jnp.full((diff_width,), q_scale, F32), jnp.ones((2 * diff_width,), F32),
        jnp.full((fox_width,), q_scale, F32), jnp.ones((2 * fox_width,), F32)]).reshape(1, qkv_cols)
    w_in_bf = w_in[0].astype(BF16)
    proj = _in_proj(n, w_in_bf[:, :qkv_cols], col_scale)
    c = _forget_cumsum(n, w_in_bf[:, qkv_cols:].T, b_forget[0])
    o_d = _diff_attention(proj, seq, n_diff, 0, diff_width, 2 * diff_width,
                          (lambda_q1[0], lambda_k1[0], lambda_q2[0], lambda_k2[0]),
                          subln_gain[0], lambda_init)
    o_f = _fox_attention(proj, c, seq, n_fox, 3 * diff_width, 3 * diff_width + fox_width,
                         3 * diff_width + 2 * fox_width)
    h = _out_proj_residual(o_d, o_f, w_out[0].astype(BF16), h)

    h = _swiglu_residual(h, seq, ffn2_norm[0], ffn2_w_gate[0], ffn2_w_up[0], ffn2_w_down[0])
    return _rmsnorm(h, final_norm, seq, x.dtype)[None]
```

```python
import functools
import math

import jax
import jax.numpy as jnp
from jax import lax
from jax.experimental import pallas as pl
from jax.experimental.pallas import tpu as pltpu

N_META = 16
META_BLOCK = 128
HEAD_DIM = 128
DIFF_V_DIM = 2 * HEAD_DIM
AUG = 128
INVALID_LANE = AUG - 1
POS_RADIX = 64
FOX_HEADS_PER_STEP = 4
DIFF_HEADS_PER_STEP = 2
RMS_EPS = 1e-6
SUBLN_EPS = 1e-5
NEG_INF = -1e30
LOG2E = math.log2(math.e)
VMEM_LIMIT_BYTES = 56 * 1024 * 1024

F32 = jnp.float32
BF16 = jnp.bfloat16
NT_DIMS = (((1,), (1,)), ((), ()))


def _pick(n, target, mult):
    best = None
    for d in range(mult, min(n, target) + 1, mult):
        if n % d == 0:
            best = d
    assert best is not None, (n, target, mult)
    return best


def _params(*sem):
    return pltpu.CompilerParams(dimension_semantics=sem, vmem_limit_bytes=VMEM_LIMIT_BYTES)


def _split3(x):
    hi = x.astype(BF16)
    rest = x - hi.astype(F32)
    mid = rest.astype(BF16)
    lo = (rest - mid.astype(F32)).astype(BF16)
    return hi, mid, lo


def _rmsnorm_kernel(h_ref, g_ref, o_ref, *, eps):
    x = h_ref[...]
    ms = jnp.mean(x * x, axis=-1, keepdims=True)
    o_ref[...] = (x * lax.rsqrt(ms + eps) * g_ref[...]).astype(o_ref.dtype)


def _rmsnorm(h, g, rows, out_dtype):
    d = h.shape[1]
    tm = _pick(rows, 520, 8)
    return pl.pallas_call(
        functools.partial(_rmsnorm_kernel, eps=RMS_EPS),
        grid=(rows // tm,),
        in_specs=[pl.BlockSpec((tm, d), lambda i: (i, 0)),
                  pl.BlockSpec((1, d), lambda i: (0, 0))],
        out_specs=pl.BlockSpec((tm, d), lambda i: (i, 0)),
        out_shape=jax.ShapeDtypeStruct((rows, d), out_dtype),
        compiler_params=_params("arbitrary"),
        name="rmsnorm",
    )(h, g.reshape(1, d).astype(F32))


def _gate_up_kernel(n_ref, wg_ref, wu_ref, o_ref):
    n = n_ref[...]
    g = jnp.dot(n, wg_ref[...], preferred_element_type=F32)
    u = jnp.dot(n, wu_ref[...], preferred_element_type=F32)
    o_ref[...] = (g * jax.nn.sigmoid(g) * u).astype(o_ref.dtype)


def _gate_up(n, wg, wu):
    rows, d = n.shape
    f = wg.shape[1]
    tm = _pick(rows, 1040, 16)
    tf = _pick(f, 256, 128)
    return pl.pallas_call(
        _gate_up_kernel,
        grid=(rows // tm, f // tf),
        in_specs=[pl.BlockSpec((tm, d), lambda i, j: (i, 0)),
                  pl.BlockSpec((d, tf), lambda i, j: (0, j)),
                  pl.BlockSpec((d, tf), lambda i, j: (0, j))],
        out_specs=pl.BlockSpec((tm, tf), lambda i, j: (i, j)),
        out_shape=jax.ShapeDtypeStruct((rows, f), BF16),
        compiler_params=_params("arbitrary", "arbitrary"),
        name="ffn_gate_up",
    )(n, wg, wu)


def _down_kernel(a_ref, wd_ref, h_ref, o_ref):
    o_ref[...] = h_ref[...] + 0.5 * jnp.dot(a_ref[...], wd_ref[...], preferred_element_type=F32)


def _down_residual(act, wd, h):
    rows, f = act.shape
    d = wd.shape[1]
    tm = _pick(rows, 520, 16)
    tn = _pick(d, 512, 128)
    return pl.pallas_call(
        _down_kernel,
        grid=(rows // tm, d // tn),
        in_specs=[pl.BlockSpec((tm, f), lambda i, j: (i, 0)),
                  pl.BlockSpec((f, tn), lambda i, j: (0, j)),
                  pl.BlockSpec((tm, tn), lambda i, j: (i, j))],
        out_specs=pl.BlockSpec((tm, tn), lambda i, j: (i, j)),
        out_shape=jax.ShapeDtypeStruct((rows, d), F32),
        compiler_params=_params("arbitrary", "arbitrary"),
        name="ffn_down",
    )(act, wd, h)


def _swiglu_residual(h, rows, norm_g, wg, wu, wd):
    n = _rmsnorm(h, norm_g, rows, BF16)
    act = _gate_up(n, wg.astype(BF16), wu.astype(BF16))
    return _down_residual(act, wd.astype(BF16), h)


def _qk_proj_kernel(n_ref, w_ref, s_ref, o_ref):
    y = jnp.dot(n_ref[...], w_ref[...], preferred_element_type=F32)
    o_ref[...] = (y * s_ref[...]).astype(o_ref.dtype)


def _qk_proj(n, w_in, col_scale, diff_width, fox_width):
    rows, d = n.shape
    cols = 2 * diff_width + 2 * fox_width
    tm = _pick(rows, 1040, 16)
    tn = _pick(math.gcd(diff_width, fox_width), 1024, 128)
    diff_blocks = 2 * diff_width // tn
    skip = diff_width // tn
    return pl.pallas_call(
        _qk_proj_kernel,
        grid=(rows // tm, cols // tn),
        in_specs=[pl.BlockSpec((tm, d), lambda i, j: (i, 0)),
                  pl.BlockSpec((d, tn), lambda i, j: (0, j + skip * (j // diff_blocks))),
                  pl.BlockSpec((1, tn), lambda i, j: (0, j))],
        out_specs=pl.BlockSpec((tm, tn), lambda i, j: (i, j)),
        out_shape=jax.ShapeDtypeStruct((rows, cols), BF16),
        compiler_params=_params("arbitrary", "arbitrary"),
        name="mixer_qk_proj",
    )(n, w_in, col_scale)


def _vt_proj_kernel(w_ref, n_ref, o_ref):
    o_ref[...] = lax.dot_general(w_ref[...], n_ref[...], NT_DIMS,
                                 preferred_element_type=F32).astype(o_ref.dtype)


def _vt_proj(w_vt, n):
    feats, d = w_vt.shape
    rows = n.shape[0]
    tc = _pick(feats, 1024, 128)
    tr = _pick(rows, 640, 128)
    return pl.pallas_call(
        _vt_proj_kernel,
        grid=(feats // tc, rows // tr),
        in_specs=[pl.BlockSpec((tc, d), lambda c, r: (c, 0)),
                  pl.BlockSpec((tr, d), lambda c, r: (r, 0))],
        out_specs=pl.BlockSpec((tc, tr), lambda c, r: (c, r)),
        out_shape=jax.ShapeDtypeStruct((feats, rows), BF16),
        compiler_params=_params("arbitrary", "arbitrary"),
        name="mixer_vt_proj",
    )(w_vt, n)


def _forget_kernel(n_ref, wf_ref, b_ref, tri_ref, spread_ref, aug_ref, carry_ref, *, n_blocks):
    blk = pl.program_id(0)

    @pl.when(blk == 0)
    def _():
        carry_ref[...] = jnp.zeros_like(carry_ref)

    logit = jnp.dot(n_ref[...], wf_ref[...], preferred_element_type=F32) + b_ref[...]
    log_f = jnp.minimum(logit, 0.0) - jnp.log1p(jnp.exp(-jnp.abs(logit)))
    is_meta = blk == n_blocks - 1
    row = lax.broadcasted_iota(jnp.int32, log_f.shape, 0)
    log_f = jnp.where(jnp.logical_and(is_meta, row >= N_META), 0.0, log_f)
    tri = tri_ref[...]
    cs = sum(jnp.dot(tri, piece, preferred_element_type=F32) for piece in _split3(log_f))
    total = cs[META_BLOCK - 1:META_BLOCK, :]
    carry = carry_ref[...]
    bias = -LOG2E * jnp.where(is_meta, cs - total, cs + carry)
    carry_ref[...] = carry + total

    lanes = sum(jnp.dot(piece, spread_ref[p], preferred_element_type=F32)
                for p, piece in enumerate(_split3(bias)))
    lane = lax.broadcasted_iota(jnp.int32, (META_BLOCK, AUG), 1)
    row = lax.broadcasted_iota(jnp.int32, (META_BLOCK, AUG), 0)
    invalid = jnp.logical_and(is_meta, row >= N_META)
    marker = jnp.where(lane == INVALID_LANE, NEG_INF, 0.0)
    aug_ref[...] = jnp.where(invalid, marker, lanes).astype(BF16)


def _forget_bias_lanes(n, wf, b_forget):
    rows, d = n.shape
    hf = wf.shape[1]
    assert 3 * hf <= INVALID_LANE
    n_blocks = rows // META_BLOCK
    tri = jnp.tril(jnp.ones((META_BLOCK, META_BLOCK), F32)).astype(BF16)
    lane = jnp.arange(AUG)[None, None, :]
    spread = (lane == 3 * jnp.arange(hf)[None, :, None] + jnp.arange(3)[:, None, None]).astype(BF16)
    return pl.pallas_call(
        functools.partial(_forget_kernel, n_blocks=n_blocks),
        grid=(n_blocks,),
        in_specs=[pl.BlockSpec((META_BLOCK, d), lambda b: (b, 0)),
                  pl.BlockSpec((d, hf), lambda b: (0, 0)),
                  pl.BlockSpec((1, hf), lambda b: (0, 0)),
                  pl.BlockSpec((META_BLOCK, META_BLOCK), lambda b: (0, 0)),
                  pl.BlockSpec((3, hf, AUG), lambda b: (0, 0, 0))],
        out_specs=pl.BlockSpec((META_BLOCK, AUG), lambda b: (b, 0)),
        out_shape=jax.ShapeDtypeStruct((rows, AUG), BF16),
        scratch_shapes=[pltpu.VMEM((1, hf), F32)],
        compiler_params=_params("arbitrary"),
        name="forget_bias_lanes",
    )(n, wf, b_forget.reshape(1, hf).astype(F32), tri, spread)


def _scores(q_cat, k_cat):
    return lax.dot_general(k_cat, q_cat, NT_DIMS, preferred_element_type=F32)


def _softmax_update(s, vt_blk, mask, m_ref, l_ref, acc_ref):
    if mask is not None:
        s = jnp.where(mask, s, NEG_INF)
    m_prev = m_ref[...]
    m_new = jnp.maximum(m_prev, jnp.max(s, axis=0, keepdims=True))
    alpha = jnp.exp2(m_prev - m_new)
    p = jnp.exp2(s - m_new)
    l_ref[...] = alpha * l_ref[...] + jnp.sum(p, axis=0, keepdims=True)
    acc_ref[...] = alpha * acc_ref[...] + jnp.dot(vt_blk, p.astype(BF16),
                                                  preferred_element_type=F32)
    m_ref[...] = m_new


def _causal_sweep(step, qi, tq, seq):
    step(seq, META_BLOCK, None)

    def body(j, carry):
        step(pl.multiple_of(j * tq, tq), tq, None)
        return carry

    lax.fori_loop(0, qi, body, 0)
    diag_mask = (lax.broadcasted_iota(jnp.int32, (tq, tq), 0)
                 <= lax.broadcasted_iota(jnp.int32, (tq, tq), 1))
    step(pl.multiple_of(qi * tq, tq), tq, diag_mask)


def _init_state(m_ref, l_ref, acc_ref):
    m_ref[...] = jnp.full_like(m_ref, NEG_INF)
    l_ref[...] = jnp.zeros_like(l_ref)
    acc_ref[...] = jnp.zeros_like(acc_ref)


def _fox_kernel(q_ref, k_ref, aug_ref, vt_ref, o_ref, m_ref, l_ref, acc_ref, qcat_ref,
                *, tq, seq, heads):
    group = pl.program_id(0)
    qi = pl.program_id(1)
    _init_state(m_ref, l_ref, acc_ref)
    lane = lax.broadcasted_iota(jnp.int32, (tq, AUG), 1)
    for a in range(heads):
        first = 3 * (group * heads + a)
        picks = jnp.logical_or(jnp.logical_and(lane >= first, lane < first + 3),
                               lane == INVALID_LANE)
        qcat_ref[a] = jnp.concatenate([q_ref[:, a * HEAD_DIM:(a + 1) * HEAD_DIM],
                                       jnp.where(picks, 1.0, 0.0).astype(BF16)], axis=1)

    def step(k0, width, mask):
        aug_blk = aug_ref[pl.ds(k0, width), :]
        head_lanes = [slice(a * HEAD_DIM, (a + 1) * HEAD_DIM) for a in range(heads)]
        scores = [_scores(qcat_ref[a], jnp.concatenate([k_ref[pl.ds(k0, width), head_lanes[a]],
                                                        aug_blk], axis=1))
                  for a in range(heads)]
        for a in range(heads):
            _softmax_update(scores[a], vt_ref[head_lanes[a], pl.ds(k0, width)], mask,
                            m_ref.at[a], l_ref.at[a], acc_ref.at[a])

    _causal_sweep(step, qi, tq, seq)
    for a in range(heads):
        o_t = acc_ref[a] / l_ref[a]
        o_ref[:, a * HEAD_DIM:(a + 1) * HEAD_DIM] = o_t.T.astype(o_ref.dtype)


def _fox_attention(qk, aug, vt, seq, n_heads, q_col, k_col, vt_row):
    rows = qk.shape[0]
    tq = _pick(seq, 512, 128)
    heads = math.gcd(n_heads, FOX_HEADS_PER_STEP)
    width = heads * HEAD_DIM
    qb, kb, vb = q_col // width, k_col // width, vt_row // width
    resident = pl.Buffered(1)
    return pl.pallas_call(
        functools.partial(_fox_kernel, tq=tq, seq=seq, heads=heads),
        grid=(n_heads // heads, seq // tq),
        in_specs=[pl.BlockSpec((tq, width), lambda g, i: (i, qb + g)),
                  pl.BlockSpec((rows, width), lambda g, i: (0, kb + g), pipeline_mode=resident),
                  pl.BlockSpec((rows, AUG), lambda g, i: (0, 0), pipeline_mode=resident),
                  pl.BlockSpec((width, rows), lambda g, i: (vb + g, 0), pipeline_mode=resident)],
        out_specs=pl.BlockSpec((tq, width), lambda g, i: (i, g)),
        out_shape=jax.ShapeDtypeStruct((seq, n_heads * HEAD_DIM), BF16),
        scratch_shapes=[pltpu.VMEM((heads, 1, tq), F32), pltpu.VMEM((heads, 1, tq), F32),
                        pltpu.VMEM((heads, HEAD_DIM, tq), F32),
                        pltpu.VMEM((heads, tq, HEAD_DIM + AUG), BF16)],
        compiler_params=_params("arbitrary", "arbitrary"),
        name="fox_attention",
    )(qk, qk, aug, vt)


def _diff_kernel(q_ref, k_ref, pos_ref, qaug_ref, vt_ref, lq1_ref, lk1_ref, lq2_ref, lk2_ref,
                 gain_ref, o_ref, m_ref, l_ref, acc_ref, qcat_ref, *, tq, seq, heads, lambda_init):
    qi = pl.program_id(1)
    _init_state(m_ref, l_ref, acc_ref)
    for a in range(heads):
        q_aug = jnp.broadcast_to(qaug_ref[a], (tq, AUG)).astype(BF16)
        for c in range(2):
            lanes = slice((2 * a + c) * HEAD_DIM, (2 * a + c + 1) * HEAD_DIM)
            qcat_ref[2 * a + c] = jnp.concatenate([q_ref[:, lanes], q_aug], axis=1)

    def step(k0, width, mask):
        pos_blk = pos_ref[pl.ds(k0, width), :]
        scores = [_scores(qcat_ref[ch], jnp.concatenate(
            [k_ref[pl.ds(k0, width), ch * HEAD_DIM:(ch + 1) * HEAD_DIM], pos_blk], axis=1))
            for ch in range(2 * heads)]
        for a in range(heads):
            vt_blk = vt_ref[a * DIFF_V_DIM:(a + 1) * DIFF_V_DIM, pl.ds(k0, width)]
            for c in range(2):
                ch = 2 * a + c
                _softmax_update(scores[ch], vt_blk, mask, m_ref.at[ch], l_ref.at[ch], acc_ref.at[ch])

    _causal_sweep(step, qi, tq, seq)
    lam = (jnp.exp(jnp.sum(lq1_ref[...] * lk1_ref[...], axis=1, keepdims=True))
           - jnp.exp(jnp.sum(lq2_ref[...] * lk2_ref[...], axis=1, keepdims=True))
           + lambda_init)
    for a in range(heads):
        o = (acc_ref[2 * a] / l_ref[2 * a] - lam * (acc_ref[2 * a + 1] / l_ref[2 * a + 1])).T
        ms = jnp.mean(o * o, axis=-1, keepdims=True)
        o = o * lax.rsqrt(ms + SUBLN_EPS) * gain_ref[...]
        o_ref[:, a * DIFF_V_DIM:(a + 1) * DIFF_V_DIM] = (o * (1.0 - lambda_init)).astype(o_ref.dtype)


def _alibi_lanes(n_heads, seq, rows):
    slopes = 2.0 ** (-8.0 * jnp.arange(1, n_heads + 1, dtype=F32) / n_heads) * LOG2E
    q_pieces = [p.astype(F32) for p in _split3(slopes * POS_RADIX)] \
        + [p.astype(F32) for p in _split3(slopes)]
    q_aug = jnp.zeros((n_heads, 1, AUG), F32).at[:, 0, INVALID_LANE].set(1.0)
    for lane_idx, piece in enumerate(q_pieces):
        q_aug = q_aug.at[:, 0, lane_idx].set(piece)
    idx = jnp.arange(rows)
    pos = jnp.where(idx < seq, idx + N_META, idx - seq)
    valid = idx < seq + N_META
    hi_digit = jnp.where(valid, pos // POS_RADIX, 0).astype(F32)
    lo_digit = jnp.where(valid, pos % POS_RADIX, 0).astype(F32)
    k_aug = jnp.zeros((rows, AUG), F32).at[:, INVALID_LANE].set(jnp.where(valid, 0.0, NEG_INF))
    for lane_idx, col in enumerate([hi_digit] * 3 + [lo_digit] * 3):
        k_aug = k_aug.at[:, lane_idx].set(col)
    return q_aug, k_aug.astype(BF16)


def _diff_attention(qk, vt, seq, n_heads, q_col, k_col, vt_row, lam_vecs, gain, lambda_init):
    rows = qk.shape[0]
    tq = _pick(seq, 512, 128)
    heads = math.gcd(n_heads, DIFF_HEADS_PER_STEP)
    width = heads * DIFF_V_DIM
    qb, kb, vb = q_col // width, k_col // width, vt_row // width
    q_aug, k_aug = _alibi_lanes(n_heads, seq, rows)
    vec = pl.BlockSpec((1, HEAD_DIM), lambda g, i: (0, 0))
    resident = pl.Buffered(1)
    return pl.pallas_call(
        functools.partial(_diff_kernel, tq=tq, seq=seq, heads=heads, lambda_init=lambda_init),
        grid=(n_heads // heads, seq // tq),
        in_specs=[pl.BlockSpec((tq, width), lambda g, i: (i, qb + g)),
                  pl.BlockSpec((rows, width), lambda g, i: (0, kb + g), pipeline_mode=resident),
                  pl.BlockSpec((rows, AUG), lambda g, i: (0, 0), pipeline_mode=resident),
                  pl.BlockSpec((heads, 1, AUG), lambda g, i: (g, 0, 0)),
                  pl.BlockSpec((width, rows), lambda g, i: (vb + g, 0), pipeline_mode=resident),
                  vec, vec, vec, vec,
                  pl.BlockSpec((1, DIFF_V_DIM), lambda g, i: (0, 0))],
        out_specs=pl.BlockSpec((tq, width), lambda g, i: (i, g)),
        out_shape=jax.ShapeDtypeStruct((seq, n_heads * DIFF_V_DIM), BF16),
        scratch_shapes=[pltpu.VMEM((2 * heads, 1, tq), F32), pltpu.VMEM((2 * heads, 1, tq), F32),
                        pltpu.VMEM((2 * heads, DIFF_V_DIM, tq), F32),
                        pltpu.VMEM((2 * heads, tq, HEAD_DIM + AUG), BF16)],
        compiler_params=_params("arbitrary", "arbitrary"),
        name="diff_attention",
    )(qk, qk, k_aug, q_aug, vt, *[v.reshape(1, HEAD_DIM).astype(F32) for v in lam_vecs],
      gain.reshape(1, DIFF_V_DIM).astype(F32))


def _out_proj_kernel(od_ref, of_ref, wd_ref, wf_ref, h_ref, o_ref):
    o_ref[...] = (h_ref[...]
                  + jnp.dot(od_ref[...], wd_ref[...], preferred_element_type=F32)
                  + jnp.dot(of_ref[...], wf_ref[...], preferred_element_type=F32))


def _out_proj_residual(o_d, o_f, w_out, h):
    seq, dw = o_d.shape
    fw = o_f.shape[1]
    d = w_out.shape[1]
    tm = _pick(seq, 1024, 16)
    tn = _pick(d, 512, 128)
    assert dw == fw
    return pl.pallas_call(
        _out_proj_kernel,
        grid=(seq // tm, d // tn),
        in_specs=[pl.BlockSpec((tm, dw), lambda i, j: (i, 0)),
                  pl.BlockSpec((tm, fw), lambda i, j: (i, 0)),
                  pl.BlockSpec((dw, tn), lambda i, j: (0, j)),
                  pl.BlockSpec((fw, tn), lambda i, j: (1, j)),
                  pl.BlockSpec((tm, tn), lambda i, j: (i, j))],
        out_specs=pl.BlockSpec((tm, tn), lambda i, j: (i, j)),
        out_shape=jax.ShapeDtypeStruct((seq, d), F32),
        compiler_params=_params("arbitrary", "arbitrary"),
        name="mixer_out_proj",
    )(o_d, o_f, w_out, w_out, h)


def kernel(x, meta_tokens, ffn1_norm, ffn1_w_gate, ffn1_w_up, ffn1_w_down, mix_norm, w_in, b_forget, lambda_q1, lambda_k1, lambda_q2, lambda_k2, subln_gain, w_out, ffn2_norm, ffn2_w_gate, ffn2_w_up, ffn2_w_down, final_norm):
    batch, seq, d = x.shape
    depth = ffn1_norm.shape[0]
    assert batch == 1 and depth == 1 and meta_tokens.shape[0] == N_META
    diff_width = d // 2
    fox_width = d - diff_width
    n_diff = diff_width // DIFF_V_DIM
    n_fox = fox_width // HEAD_DIM
    qkv_cols = 3 * diff_width + 3 * fox_width
    lambda_init = 0.8 - 0.6 * math.exp(-0.3 * 0)

    rows = seq + META_BLOCK
    h = jnp.concatenate(
        [x[0], meta_tokens.astype(x.dtype), jnp.zeros((META_BLOCK - N_META, d), x.dtype)], axis=0)

    h = _swiglu_residual(h, rows, ffn1_norm[0], ffn1_w_gate[0], ffn1_w_up[0], ffn1_w_down[0])

    n = _rmsnorm(h, mix_norm[0], rows, BF16)
    q_scale = HEAD_DIM ** -0.5 * LOG2E
    col_scale = jnp.concatenate([
        jnp.full((diff_width,), q_scale, F32), jnp.ones((diff_width,), F32),
        jnp.full((fox_width,), q_scale, F32), jnp.ones((fox_width,), F32)]).reshape(1, -1)
    w_in_bf = w_in[0].astype(BF16)
    qk = _qk_proj(n, w_in_bf, col_scale, diff_width, fox_width)
    w_vt = jnp.concatenate([w_in_bf[:, 2 * diff_width:3 * diff_width],
                            w_in_bf[:, 3 * diff_width + 2 * fox_width:qkv_cols]], axis=1).T
    vt = _vt_proj(w_vt, n)
    aug = _forget_bias_lanes(n, w_in_bf[:, qkv_cols:], b_forget[0])
    o_d = _diff_attention(qk, vt, seq, n_diff, 0, diff_width, 0,
                          (lambda_q1[0], lambda_k1[0], lambda_q2[0], lambda_k2[0]),
                          subln_gain[0], lambda_init)
    o_f = _fox_attention(qk, aug, vt, seq, n_fox, 2 * diff_width, 2 * diff_width + fox_width,
                         diff_width)
    h = _out_proj_residual(o_d, o_f, w_out[0].astype(BF16), h)

    h = _swiglu_residual(h, seq, ffn2_norm[0], ffn2_w_gate[0], ffn2_w_up[0], ffn2_w_down[0])
    return _rmsnorm(h, final_norm, seq, x.dtype)[None]
```

```python
import functools
import math

import jax
import jax.numpy as jnp
from jax import lax
from jax.experimental import pallas as pl
from jax.experimental.pallas import tpu as pltpu

N_META = 16
META_BLOCK = 128
HEAD_DIM = 128
DIFF_V_DIM = 2 * HEAD_DIM
AUG = 128
INVALID_LANE = AUG - 1
POS_RADIX = 64
FOX_HEADS_PER_STEP = 4
DIFF_HEADS_PER_STEP = 2
RMS_EPS = 1e-6
SUBLN_EPS = 1e-5
NEG_INF = -1e30
LOG2E = math.log2(math.e)
VMEM_LIMIT_BYTES = 56 * 1024 * 1024

F32 = jnp.float32
BF16 = jnp.bfloat16
NT_DIMS = (((1,), (1,)), ((), ()))


def _pick(n, target, mult):
    best = None
    for d in range(mult, min(n, target) + 1, mult):
        if n % d == 0:
            best = d
    assert best is not None, (n, target, mult)
    return best


def _params(*sem):
    return pltpu.CompilerParams(dimension_semantics=sem, vmem_limit_bytes=VMEM_LIMIT_BYTES)


def _split3(x):
    hi = x.astype(BF16)
    rest = x - hi.astype(F32)
    mid = rest.astype(BF16)
    lo = (rest - mid.astype(F32)).astype(BF16)
    return hi, mid, lo


def _rmsnorm_kernel(h_ref, g_ref, o_ref, *, eps):
    x = h_ref[...]
    ms = jnp.mean(x * x, axis=-1, keepdims=True)
    o_ref[...] = (x * lax.rsqrt(ms + eps) * g_ref[...]).astype(o_ref.dtype)


def _rmsnorm(h, g, rows, out_dtype):
    d = h.shape[1]
    tm = _pick(rows, 520, 8)
    return pl.pallas_call(
        functools.partial(_rmsnorm_kernel, eps=RMS_EPS),
        grid=(rows // tm,),
        in_specs=[pl.BlockSpec((tm, d), lambda i: (i, 0)),
                  pl.BlockSpec((1, d), lambda i: (0, 0))],
        out_specs=pl.BlockSpec((tm, d), lambda i: (i, 0)),
        out_shape=jax.ShapeDtypeStruct((rows, d), out_dtype),
        compiler_params=_params("arbitrary"),
        name="rmsnorm",
    )(h, g.reshape(1, d).astype(F32))


def _gate_up_kernel(n_ref, wg_ref, wu_ref, o_ref):
    n = n_ref[...]
    g = jnp.dot(n, wg_ref[...].astype(BF16), preferred_element_type=F32)
    u = jnp.dot(n, wu_ref[...].astype(BF16), preferred_element_type=F32)
    o_ref[...] = (g * jax.nn.sigmoid(g) * u).astype(o_ref.dtype)


def _gate_up(n, wg, wu):
    rows, d = n.shape
    f = wg.shape[1]
    tm = _pick(rows, 1040, 16)
    tf = _pick(f, 256, 128)
    return pl.pallas_call(
        _gate_up_kernel,
        grid=(rows // tm, f // tf),
        in_specs=[pl.BlockSpec((tm, d), lambda i, j: (i, 0)),
                  pl.BlockSpec((d, tf), lambda i, j: (0, j)),
                  pl.BlockSpec((d, tf), lambda i, j: (0, j))],
        out_specs=pl.BlockSpec((tm, tf), lambda i, j: (i, j)),
        out_shape=jax.ShapeDtypeStruct((rows, f), BF16),
        compiler_params=_params("arbitrary", "arbitrary"),
        name="ffn_gate_up",
    )(n, wg, wu)


def _down_kernel(a_ref, wd_ref, h_ref, o_ref):
    o_ref[...] = h_ref[...] + 0.5 * jnp.dot(a_ref[...], wd_ref[...], preferred_element_type=F32)


def _down_residual(act, wd, h):
    rows, f = act.shape
    d = wd.shape[1]
    tm = _pick(rows, 520, 16)
    tn = _pick(d, 512, 128)
    return pl.pallas_call(
        _down_kernel,
        grid=(rows // tm, d // tn),
        in_specs=[pl.BlockSpec((tm, f), lambda i, j: (i, 0)),
                  pl.BlockSpec((f, tn), lambda i, j: (0, j)),
                  pl.BlockSpec((tm, tn), lambda i, j: (i, j))],
        out_specs=pl.BlockSpec((tm, tn), lambda i, j: (i, j)),
        out_shape=jax.ShapeDtypeStruct((rows, d), F32),
        compiler_params=_params("arbitrary", "arbitrary"),
        name="ffn_down",
    )(act, wd, h)


def _swiglu_residual(h, rows, norm_g, wg, wu, wd):
    n = _rmsnorm(h, norm_g, rows, BF16)
    act = _gate_up(n, wg, wu)
    return _down_residual(act, wd.astype(BF16), h)


def _qk_proj_kernel(n_ref, w_ref, s_ref, o_ref):
    y = jnp.dot(n_ref[...], w_ref[...].astype(BF16), preferred_element_type=F32)
    o_ref[...] = (y * s_ref[...]).astype(o_ref.dtype)


def _qk_proj(n, w_in, col_scale, diff_width, fox_width):
    rows, d = n.shape
    cols = 2 * diff_width + 2 * fox_width
    tm = _pick(rows, 1040, 16)
    tn = _pick(math.gcd(diff_width, fox_width), 512, 128)
    diff_blocks = 2 * diff_width // tn
    skip = diff_width // tn
    return pl.pallas_call(
        _qk_proj_kernel,
        grid=(rows // tm, cols // tn),
        in_specs=[pl.BlockSpec((tm, d), lambda i, j: (i, 0)),
                  pl.BlockSpec((d, tn), lambda i, j: (0, j + skip * (j // diff_blocks))),
                  pl.BlockSpec((1, tn), lambda i, j: (0, j))],
        out_specs=pl.BlockSpec((tm, tn), lambda i, j: (i, j)),
        out_shape=jax.ShapeDtypeStruct((rows, cols), BF16),
        compiler_params=_params("arbitrary", "arbitrary"),
        name="mixer_qk_proj",
    )(n, w_in, col_scale)


def _vt_proj_kernel(w_ref, n_ref, o_ref):
    o_ref[...] = lax.dot_general(w_ref[...], n_ref[...], NT_DIMS,
                                 preferred_element_type=F32).astype(o_ref.dtype)


def _vt_proj(w_vt, n):
    feats, d = w_vt.shape
    rows = n.shape[0]
    tc = _pick(feats, 1024, 128)
    tr = _pick(rows, 640, 128)
    return pl.pallas_call(
        _vt_proj_kernel,
        grid=(feats // tc, rows // tr),
        in_specs=[pl.BlockSpec((tc, d), lambda c, r: (c, 0)),
                  pl.BlockSpec((tr, d), lambda c, r: (r, 0))],
        out_specs=pl.BlockSpec((tc, tr), lambda c, r: (c, r)),
        out_shape=jax.ShapeDtypeStruct((feats, rows), BF16),
        compiler_params=_params("arbitrary", "arbitrary"),
        name="mixer_vt_proj",
    )(w_vt, n)


def _forget_kernel(n_ref, wf_ref, b_ref, tri_ref, spread_ref, aug_ref, carry_ref, *, n_blocks):
    blk = pl.program_id(0)

    @pl.when(blk == 0)
    def _():
        carry_ref[...] = jnp.zeros_like(carry_ref)

    logit = jnp.dot(n_ref[...], wf_ref[...], preferred_element_type=F32) + b_ref[...]
    log_f = jnp.minimum(logit, 0.0) - jnp.log1p(jnp.exp(-jnp.abs(logit)))
    is_meta = blk == n_blocks - 1
    row = lax.broadcasted_iota(jnp.int32, log_f.shape, 0)
    log_f = jnp.where(jnp.logical_and(is_meta, row >= N_META), 0.0, log_f)
    tri = tri_ref[...]
    cs = sum(jnp.dot(tri, piece, preferred_element_type=F32) for piece in _split3(log_f))
    total = cs[META_BLOCK - 1:META_BLOCK, :]
    carry = carry_ref[...]
    bias = -LOG2E * jnp.where(is_meta, cs - total, cs + carry)
    carry_ref[...] = carry + total

    lanes = sum(jnp.dot(piece, spread_ref[p], preferred_element_type=F32)
                for p, piece in enumerate(_split3(bias)))
    lane = lax.broadcasted_iota(jnp.int32, (META_BLOCK, AUG), 1)
    row = lax.broadcasted_iota(jnp.int32, (META_BLOCK, AUG), 0)
    invalid = jnp.logical_and(is_meta, row >= N_META)
    marker = jnp.where(lane == INVALID_LANE, NEG_INF, 0.0)
    aug_ref[...] = jnp.where(invalid, marker, lanes).astype(BF16)


def _forget_bias_lanes(n, wf, b_forget):
    rows, d = n.shape
    hf = wf.shape[1]
    assert 3 * hf <= INVALID_LANE
    n_blocks = rows // META_BLOCK
    tri = jnp.tril(jnp.ones((META_BLOCK, META_BLOCK), F32)).astype(BF16)
    lane = jnp.arange(AUG)[None, None, :]
    spread = (lane == 3 * jnp.arange(hf)[None, :, None] + jnp.arange(3)[:, None, None]).astype(BF16)
    return pl.pallas_call(
        functools.partial(_forget_kernel, n_blocks=n_blocks),
        grid=(n_blocks,),
        in_specs=[pl.BlockSpec((META_BLOCK, d), lambda b: (b, 0)),
                  pl.BlockSpec((d, hf), lambda b: (0, 0)),
                  pl.BlockSpec((1, hf), lambda b: (0, 0)),
                  pl.BlockSpec((META_BLOCK, META_BLOCK), lambda b: (0, 0)),
                  pl.BlockSpec((3, hf, AUG), lambda b: (0, 0, 0))],
        out_specs=pl.BlockSpec((META_BLOCK, AUG), lambda b: (b, 0)),
        out_shape=jax.ShapeDtypeStruct((rows, AUG), BF16),
        scratch_shapes=[pltpu.VMEM((1, hf), F32)],
        compiler_params=_params("arbitrary"),
        name="forget_bias_lanes",
    )(n, wf, b_forget.reshape(1, hf).astype(F32), tri, spread)


def _scores(q_cat, k_cat):
    return lax.dot_general(k_cat, q_cat, NT_DIMS, preferred_element_type=F32)


def _softmax_update(s, vt_blk, mask, m_ref, l_ref, acc_ref):
    if mask is not None:
        s = jnp.where(mask, s, NEG_INF)
    m_prev = m_ref[...]
    m_new = jnp.maximum(m_prev, jnp.max(s, axis=0, keepdims=True))
    alpha = jnp.exp2(m_prev - m_new)
    p = jnp.exp2(s - m_new)
    l_ref[...] = alpha * l_ref[...] + jnp.sum(p, axis=0, keepdims=True)
    acc_ref[...] = alpha * acc_ref[...] + jnp.dot(vt_blk, p.astype(BF16),
                                                  preferred_element_type=F32)
    m_ref[...] = m_new


def _causal_sweep(score, consume, sa_ref, sb_ref, qi, tq, seq):
    def start(j):
        return pl.multiple_of(j * tq, tq)

    score(seq, META_BLOCK, sb_ref)
    score(start(0), tq, sa_ref)
    consume(sb_ref, seq, META_BLOCK, None)

    def pair(p, carry):
        j = 2 * p
        score(start(j + 1), tq, sb_ref)
        consume(sa_ref, start(j), tq, None)
        score(start(j + 2), tq, sa_ref)
        consume(sb_ref, start(j + 1), tq, None)
        return carry

    lax.fori_loop(0, qi // 2, pair, 0)
    diag_mask = (lax.broadcasted_iota(jnp.int32, (tq, tq), 0)
                 <= lax.broadcasted_iota(jnp.int32, (tq, tq), 1))
    odd = qi % 2 == 1

    @pl.when(odd)
    def _():
        score(start(qi), tq, sb_ref)
        consume(sa_ref, start(qi - 1), tq, None)
        consume(sb_ref, start(qi), tq, diag_mask)

    @pl.when(jnp.logical_not(odd))
    def _():
        consume(sa_ref, start(qi), tq, diag_mask)


def _init_state(m_ref, l_ref, acc_ref):
    m_ref[...] = jnp.full_like(m_ref, NEG_INF)
    l_ref[...] = jnp.zeros_like(l_ref)
    acc_ref[...] = jnp.zeros_like(acc_ref)


def _fox_kernel(q_ref, k_ref, aug_ref, vt_ref, o_ref, m_ref, l_ref, acc_ref, qcat_ref,
                sa_ref, sb_ref, *, tq, seq, heads):
    group = pl.program_id(0)
    qi = pl.program_id(1)
    _init_state(m_ref, l_ref, acc_ref)
    lane = lax.broadcasted_iota(jnp.int32, (tq, AUG), 1)
    for a in range(heads):
        first = 3 * (group * heads + a)
        picks = jnp.logical_or(jnp.logical_and(lane >= first, lane < first + 3),
                               lane == INVALID_LANE)
        qcat_ref[a] = jnp.concatenate([q_ref[:, a * HEAD_DIM:(a + 1) * HEAD_DIM],
                                       jnp.where(picks, 1.0, 0.0).astype(BF16)], axis=1)

    head_lanes = [slice(a * HEAD_DIM, (a + 1) * HEAD_DIM) for a in range(heads)]

    def score(k0, width, s_ref):
        aug_blk = aug_ref[pl.ds(k0, width), :]
        for a in range(heads):
            k_cat = jnp.concatenate([k_ref[pl.ds(k0, width), head_lanes[a]], aug_blk], axis=1)
            s_ref[a, :width, :] = _scores(qcat_ref[a], k_cat)

    def consume(s_ref, k0, width, mask):
        for a in range(heads):
            _softmax_update(s_ref[a, :width, :], vt_ref[head_lanes[a], pl.ds(k0, width)], mask,
                            m_ref.at[a], l_ref.at[a], acc_ref.at[a])

    _causal_sweep(score, consume, sa_ref, sb_ref, qi, tq, seq)
    for a in range(heads):
        o_t = acc_ref[a] / l_ref[a]
        o_ref[:, a * HEAD_DIM:(a + 1) * HEAD_DIM] = o_t.T.astype(o_ref.dtype)


def _fox_attention(qk, aug, vt, seq, n_heads, q_col, k_col, vt_row):
    rows = qk.shape[0]
    tq = _pick(seq, 512, 128)
    heads = math.gcd(n_heads, FOX_HEADS_PER_STEP)
    width = heads * HEAD_DIM
    qb, kb, vb = q_col // width, k_col // width, vt_row // width
    resident = pl.Buffered(1)
    return pl.pallas_call(
        functools.partial(_fox_kernel, tq=tq, seq=seq, heads=heads),
        grid=(n_heads // heads, seq // tq),
        in_specs=[pl.BlockSpec((tq, width), lambda g, i: (i, qb + g)),
                  pl.BlockSpec((rows, width), lambda g, i: (0, kb + g), pipeline_mode=resident),
                  pl.BlockSpec((rows, AUG), lambda g, i: (0, 0), pipeline_mode=resident),
                  pl.BlockSpec((width, rows), lambda g, i: (vb + g, 0), pipeline_mode=resident)],
        out_specs=pl.BlockSpec((tq, width), lambda g, i: (i, g)),
        out_shape=jax.ShapeDtypeStruct((seq, n_heads * HEAD_DIM), BF16),
        scratch_shapes=[pltpu.VMEM((heads, 1, tq), F32), pltpu.VMEM((heads, 1, tq), F32),
                        pltpu.VMEM((heads, HEAD_DIM, tq), F32),
                        pltpu.VMEM((heads, tq, HEAD_DIM + AUG), BF16),
                        pltpu.VMEM((heads, tq, tq), F32), pltpu.VMEM((heads, tq, tq), F32)],
        compiler_params=_params("arbitrary", "arbitrary"),
        name="fox_attention",
    )(qk, qk, aug, vt)


def _diff_kernel(q_ref, k_ref, pos_ref, qaug_ref, vt_ref, lq1_ref, lk1_ref, lq2_ref, lk2_ref,
                 gain_ref, o_ref, m_ref, l_ref, acc_ref, qcat_ref, sa_ref, sb_ref,
                 *, tq, seq, heads, lambda_init):
    qi = pl.program_id(1)
    _init_state(m_ref, l_ref, acc_ref)
    for a in range(heads):
        q_aug = jnp.broadcast_to(qaug_ref[a], (tq, AUG)).astype(BF16)
        for c in range(2):
            lanes = slice((2 * a + c) * HEAD_DIM, (2 * a + c + 1) * HEAD_DIM)
            qcat_ref[2 * a + c] = jnp.concatenate([q_ref[:, lanes], q_aug], axis=1)

    def score(k0, width, s_ref):
        pos_blk = pos_ref[pl.ds(k0, width), :]
        for ch in range(2 * heads):
            k_cat = jnp.concatenate(
                [k_ref[pl.ds(k0, width), ch * HEAD_DIM:(ch + 1) * HEAD_DIM], pos_blk], axis=1)
            s_ref[ch, :width, :] = _scores(qcat_ref[ch], k_cat)

    def consume(s_ref, k0, width, mask):
        for a in range(heads):
            vt_blk = vt_ref[a * DIFF_V_DIM:(a + 1) * DIFF_V_DIM, pl.ds(k0, width)]
            for ch in (2 * a, 2 * a + 1):
                _softmax_update(s_ref[ch, :width, :], vt_blk, mask,
                                m_ref.at[ch], l_ref.at[ch], acc_ref.at[ch])

    _causal_sweep(score, consume, sa_ref, sb_ref, qi, tq, seq)
    lam = (jnp.exp(jnp.sum(lq1_ref[...] * lk1_ref[...], axis=1, keepdims=True))
           - jnp.exp(jnp.sum(lq2_ref[...] * lk2_ref[...], axis=1, keepdims=True))
           + lambda_init)
    for a in range(heads):
        o = (acc_ref[2 * a] / l_ref[2 * a] - lam * (acc_ref[2 * a + 1] / l_ref[2 * a + 1])).T
        ms = jnp.mean(o * o, axis=-1, keepdims=True)
        o = o * lax.rsqrt(ms + SUBLN_EPS) * gain_ref[...]
        o_ref[:, a * DIFF_V_DIM:(a + 1) * DIFF_V_DIM] = (o * (1.0 - lambda_init)).astype(o_ref.dtype)


def _alibi_lanes(n_heads, seq, rows):
    slopes = 2.0 ** (-8.0 * jnp.arange(1, n_heads + 1, dtype=F32) / n_heads) * LOG2E
    q_pieces = [p.astype(F32) for p in _split3(slopes * POS_RADIX)] \
        + [p.astype(F32) for p in _split3(slopes)]
    q_aug = jnp.zeros((n_heads, 1, AUG), F32).at[:, 0, INVALID_LANE].set(1.0)
    for lane_idx, piece in enumerate(q_pieces):
        q_aug = q_aug.at[:, 0, lane_idx].set(piece)
    idx = jnp.arange(rows)
    pos = jnp.where(idx < seq, idx + N_META, idx - seq)
    valid = idx < seq + N_META
    hi_digit = jnp.where(valid, pos // POS_RADIX, 0).astype(F32)
    lo_digit = jnp.where(valid, pos % POS_RADIX, 0).astype(F32)
    k_aug = jnp.zeros((rows, AUG), F32).at[:, INVALID_LANE].set(jnp.where(valid, 0.0, NEG_INF))
    for lane_idx, col in enumerate([hi_digit] * 3 + [lo_digit] * 3):
        k_aug = k_aug.at[:, lane_idx].set(col)
    return q_aug, k_aug.astype(BF16)


def _diff_attention(qk, vt, seq, n_heads, q_col, k_col, vt_row, lam_vecs, gain, lambda_init):
    rows = qk.shape[0]
    tq = _pick(seq, 512, 128)
    heads = math.gcd(n_heads, DIFF_HEADS_PER_STEP)
    width = heads * DIFF_V_DIM
    qb, kb, vb = q_col // width, k_col // width, vt_row // width
    q_aug, k_aug = _alibi_lanes(n_heads, seq, rows)
    vec = pl.BlockSpec((1, HEAD_DIM), lambda g, i: (0, 0))
    resident = pl.Buffered(1)
    return pl.pallas_call(
        functools.partial(_diff_kernel, tq=tq, seq=seq, heads=heads, lambda_init=lambda_init),
        grid=(n_heads // heads, seq // tq),
        in_specs=[pl.BlockSpec((tq, width), lambda g, i: (i, qb + g)),
                  pl.BlockSpec((rows, width), lambda g, i: (0, kb + g), pipeline_mode=resident),
                  pl.BlockSpec((rows, AUG), lambda g, i: (0, 0), pipeline_mode=resident),
                  pl.BlockSpec((heads, 1, AUG), lambda g, i: (g, 0, 0)),
                  pl.BlockSpec((width, rows), lambda g, i: (vb + g, 0), pipeline_mode=resident),
                  vec, vec, vec, vec,
                  pl.BlockSpec((1, DIFF_V_DIM), lambda g, i: (0, 0))],
        out_specs=pl.BlockSpec((tq, width), lambda g, i: (i, g)),
        out_shape=jax.ShapeDtypeStruct((seq, n_heads * DIFF_V_DIM), BF16),
        scratch_shapes=[pltpu.VMEM((2 * heads, 1, tq), F32), pltpu.VMEM((2 * heads, 1, tq), F32),
                        pltpu.VMEM((2 * heads, DIFF_V_DIM, tq), F32),
                        pltpu.VMEM((2 * heads, tq, HEAD_DIM + AUG), BF16),
                        pltpu.VMEM((2 * heads, tq, tq), F32), pltpu.VMEM((2 * heads, tq, tq), F32)],
        compiler_params=_params("arbitrary", "arbitrary"),
        name="diff_attention",
    )(qk, qk, k_aug, q_aug, vt, *[v.reshape(1, HEAD_DIM).astype(F32) for v in lam_vecs],
      gain.reshape(1, DIFF_V_DIM).astype(F32))


def _out_proj_kernel(od_ref, of_ref, wd_ref, wf_ref, h_ref, o_ref):
    o_ref[...] = (h_ref[...]
                  + jnp.dot(od_ref[...], wd_ref[...].astype(BF16), preferred_element_type=F32)
                  + jnp.dot(of_ref[...], wf_ref[...].astype(BF16), preferred_element_type=F32))


def _out_proj_residual(o_d, o_f, w_out, h):
    seq, dw = o_d.shape
    fw = o_f.shape[1]
    d = w_out.shape[1]
    tm = _pick(seq, 1024, 16)
    tn = _pick(d, 512, 128)
    assert dw == fw
    return pl.pallas_call(
        _out_proj_kernel,
        grid=(seq // tm, d // tn),
        in_specs=[pl.BlockSpec((tm, dw), lambda i, j: (i, 0)),
                  pl.BlockSpec((tm, fw), lambda i, j: (i, 0)),
                  pl.BlockSpec((dw, tn), lambda i, j: (0, j)),
                  pl.BlockSpec((fw, tn), lambda i, j: (1, j)),
                  pl.BlockSpec((tm, tn), lambda i, j: (i, j))],
        out_specs=pl.BlockSpec((tm, tn), lambda i, j: (i, j)),
        out_shape=jax.ShapeDtypeStruct((seq, d), F32),
        compiler_params=_params("arbitrary", "arbitrary"),
        name="mixer_out_proj",
    )(o_d, o_f, w_out, w_out, h)


def kernel(x, meta_tokens, ffn1_norm, ffn1_w_gate, ffn1_w_up, ffn1_w_down, mix_norm, w_in, b_forget, lambda_q1, lambda_k1, lambda_q2, lambda_k2, subln_gain, w_out, ffn2_norm, ffn2_w_gate, ffn2_w_up, ffn2_w_down, final_norm):
    batch, seq, d = x.shape
    depth = ffn1_norm.shape[0]
    assert batch == 1 and depth == 1 and meta_tokens.shape[0] == N_META
    diff_width = d // 2
    fox_width = d - diff_width
    n_diff = diff_width // DIFF_V_DIM
    n_fox = fox_width // HEAD_DIM
    qkv_cols = 3 * diff_width + 3 * fox_width
    lambda_init = 0.8 - 0.6 * math.exp(-0.3 * 0)

    rows = seq + META_BLOCK
    h = jnp.concatenate(
        [x[0], meta_tokens.astype(x.dtype), jnp.zeros((META_BLOCK - N_META, d), x.dtype)], axis=0)

    h = _swiglu_residual(h, rows, ffn1_norm[0], ffn1_w_gate[0], ffn1_w_up[0], ffn1_w_down[0])

    n = _rmsnorm(h, mix_norm[0], rows, BF16)
    q_scale = HEAD_DIM ** -0.5 * LOG2E
    col_scale = jnp.concatenate([
        jnp.full((diff_width,), q_scale, F32), jnp.ones((diff_width,), F32),
        jnp.full((fox_width,), q_scale, F32), jnp.ones((fox_width,), F32)]).reshape(1, -1)
    qk = _qk_proj(n, w_in[0], col_scale, diff_width, fox_width)
    w_vt = jnp.concatenate([w_in[0, :, 2 * diff_width:3 * diff_width].T,
                            w_in[0, :, 3 * diff_width + 2 * fox_width:qkv_cols].T],
                           axis=0).astype(BF16)
    vt = _vt_proj(w_vt, n)
    aug = _forget_bias_lanes(n, w_in[0, :, qkv_cols:].astype(BF16), b_forget[0])
    o_d = _diff_attention(qk, vt, seq, n_diff, 0, diff_width, 0,
                          (lambda_q1[0], lambda_k1[0], lambda_q2[0], lambda_k2[0]),
                          subln_gain[0], lambda_init)
    o_f = _fox_attention(qk, aug, vt, seq, n_fox, 2 * diff_width, 2 * diff_width + fox_width,
                         diff_width)
    h = _out_proj_residual(o_d, o_f, w_out[0], h)

    h = _swiglu_residual(h, seq, ffn2_norm[0], ffn2_w_gate[0], ffn2_w_up[0], ffn2_w_down[0])
    return _rmsnorm(h, final_norm, seq, x.dtype)[None]
```

```python
import functools
import math

import jax
import jax.numpy as jnp
import ml_dtypes
import numpy as np
from jax import lax
from jax.experimental import pallas as pl
from jax.experimental.pallas import tpu as pltpu

N_META = 16
META_BLOCK = 128
HEAD_DIM = 128
DIFF_V_DIM = 2 * HEAD_DIM
AUG = 128
INVALID_LANE = AUG - 1
POS_RADIX = 64
FOX_HEADS_PER_STEP = 4
DIFF_HEADS_PER_STEP = 2
RMS_EPS = 1e-6
SUBLN_EPS = 1e-5
NEG_INF = -1e30
LOG2E = math.log2(math.e)
VMEM_LIMIT_BYTES = 56 * 1024 * 1024

F32 = jnp.float32
BF16 = jnp.bfloat16
NT_DIMS = (((1,), (1,)), ((), ()))


def _pick(n, target, mult):
    best = None
    for d in range(mult, min(n, target) + 1, mult):
        if n % d == 0:
            best = d
    assert best is not None, (n, target, mult)
    return best


def _params(*sem):
    return pltpu.CompilerParams(dimension_semantics=sem, vmem_limit_bytes=VMEM_LIMIT_BYTES)


def _split3(x):
    hi = x.astype(BF16)
    rest = x - hi.astype(F32)
    mid = rest.astype(BF16)
    lo = (rest - mid.astype(F32)).astype(BF16)
    return hi, mid, lo


def _rmsnorm_kernel(h_ref, g_ref, o_ref, *, eps):
    x = h_ref[...]
    ms = jnp.mean(x * x, axis=-1, keepdims=True)
    o_ref[...] = (x * lax.rsqrt(ms + eps) * g_ref[...]).astype(o_ref.dtype)


def _rmsnorm(h, g, rows, out_dtype):
    d = h.shape[1]
    tm = _pick(rows, 520, 8)
    return pl.pallas_call(
        functools.partial(_rmsnorm_kernel, eps=RMS_EPS),
        grid=(rows // tm,),
        in_specs=[pl.BlockSpec((tm, d), lambda i: (i, 0)),
                  pl.BlockSpec((1, d), lambda i: (0, 0))],
        out_specs=pl.BlockSpec((tm, d), lambda i: (i, 0)),
        out_shape=jax.ShapeDtypeStruct((rows, d), out_dtype),
        compiler_params=_params("arbitrary"),
        name="rmsnorm",
    )(h, g.reshape(1, d).astype(F32))


def _gate_up_kernel(n_ref, wg_ref, wu_ref, o_ref):
    n = n_ref[...]
    g = jnp.dot(n, wg_ref[...].astype(BF16), preferred_element_type=F32)
    u = jnp.dot(n, wu_ref[...].astype(BF16), preferred_element_type=F32)
    o_ref[...] = (g * jax.nn.sigmoid(g) * u).astype(o_ref.dtype)


def _gate_up(n, wg, wu):
    rows, d = n.shape
    f = wg.shape[1]
    tm = _pick(rows, 1040, 16)
    tf = _pick(f, 256, 128)
    return pl.pallas_call(
        _gate_up_kernel,
        grid=(rows // tm, f // tf),
        in_specs=[pl.BlockSpec((tm, d), lambda i, j: (i, 0)),
                  pl.BlockSpec((d, tf), lambda i, j: (0, j)),
                  pl.BlockSpec((d, tf), lambda i, j: (0, j))],
        out_specs=pl.BlockSpec((tm, tf), lambda i, j: (i, j)),
        out_shape=jax.ShapeDtypeStruct((rows, f), BF16),
        compiler_params=_params("arbitrary", "arbitrary"),
        name="ffn_gate_up",
    )(n, wg, wu)


def _down_kernel(a_ref, wd_ref, h_ref, o_ref):
    o_ref[...] = h_ref[...] + 0.5 * jnp.dot(a_ref[...], wd_ref[...], preferred_element_type=F32)


def _down_residual(act, wd, h):
    rows, f = act.shape
    d = wd.shape[1]
    tm = _pick(rows, 520, 16)
    tn = _pick(d, 512, 128)
    return pl.pallas_call(
        _down_kernel,
        grid=(rows // tm, d // tn),
        in_specs=[pl.BlockSpec((tm, f), lambda i, j: (i, 0)),
                  pl.BlockSpec((f, tn), lambda i, j: (0, j)),
                  pl.BlockSpec((tm, tn), lambda i, j: (i, j))],
        out_specs=pl.BlockSpec((tm, tn), lambda i, j: (i, j)),
        out_shape=jax.ShapeDtypeStruct((rows, d), F32),
        compiler_params=_params("arbitrary", "arbitrary"),
        name="ffn_down",
    )(act, wd, h)


def _swiglu_residual(h, rows, norm_g, wg, wu, wd):
    n = _rmsnorm(h, norm_g, rows, BF16)
    act = _gate_up(n, wg, wu)
    return _down_residual(act, wd.astype(BF16), h)


def _qk_proj_kernel(n_ref, w_ref, s_ref, o_ref):
    y = jnp.dot(n_ref[...], w_ref[...].astype(BF16), preferred_element_type=F32)
    o_ref[...] = (y * s_ref[...]).astype(o_ref.dtype)


def _qk_proj(n, w_in, col_scale, diff_width, fox_width):
    rows, d = n.shape
    cols = 2 * diff_width + 2 * fox_width
    tm = _pick(rows, 1040, 16)
    tn = _pick(math.gcd(diff_width, fox_width), 512, 128)
    diff_blocks = 2 * diff_width // tn
    skip = diff_width // tn
    return pl.pallas_call(
        _qk_proj_kernel,
        grid=(rows // tm, cols // tn),
        in_specs=[pl.BlockSpec((tm, d), lambda i, j: (i, 0)),
                  pl.BlockSpec((d, tn), lambda i, j: (0, j + skip * (j // diff_blocks))),
                  pl.BlockSpec((1, tn), lambda i, j: (0, j))],
        out_specs=pl.BlockSpec((tm, tn), lambda i, j: (i, j)),
        out_shape=jax.ShapeDtypeStruct((rows, cols), BF16),
        compiler_params=_params("arbitrary", "arbitrary"),
        name="mixer_qk_proj",
    )(n, w_in, col_scale)


def _vt_proj_kernel(w_ref, n_ref, o_ref, wbf_ref, v_ref):
    @pl.when(pl.program_id(1) == 0)
    def _():
        wbf_ref[...] = w_ref[...].astype(BF16)

    v_ref[...] = jnp.dot(n_ref[...], wbf_ref[...], preferred_element_type=F32)
    o_ref[...] = v_ref[...].T.astype(o_ref.dtype)


def _vt_proj(n, w_in, diff_width, fox_width):
    rows, d = n.shape
    feats = diff_width + fox_width
    tc = _pick(math.gcd(diff_width, fox_width), 512, 128)
    tr = _pick(rows, 640, 128)
    diff_blocks = diff_width // tc
    first_d = 2 * diff_width // tc
    first_f = (3 * diff_width + 2 * fox_width) // tc

    def w_block(c, r):
        return 0, jnp.where(c < diff_blocks, first_d + c, first_f + c - diff_blocks)

    return pl.pallas_call(
        _vt_proj_kernel,
        grid=(feats // tc, rows // tr),
        in_specs=[pl.BlockSpec((d, tc), w_block),
                  pl.BlockSpec((tr, d), lambda c, r: (r, 0))],
        out_specs=pl.BlockSpec((tc, tr), lambda c, r: (c, r)),
        out_shape=jax.ShapeDtypeStruct((feats, rows), BF16),
        scratch_shapes=[pltpu.VMEM((d, tc), BF16), pltpu.VMEM((tr, tc), F32)],
        compiler_params=_params("arbitrary", "arbitrary"),
        name="mixer_vt_proj",
    )(w_in, n)


def _forget_kernel(n_ref, wf_ref, b_ref, tri_ref, spread_ref, aug_ref, carry_ref, wfb_ref,
                   *, n_blocks):
    blk = pl.program_id(0)
    hf = wfb_ref.shape[1]

    @pl.when(blk == 0)
    def _():
        carry_ref[...] = jnp.zeros_like(carry_ref)
        wfb_ref[...] = wf_ref[:, :hf].astype(BF16)

    logit = jnp.dot(n_ref[...], wfb_ref[...], preferred_element_type=F32) + b_ref[...]
    log_f = jnp.minimum(logit, 0.0) - jnp.log1p(jnp.exp(-jnp.abs(logit)))
    is_meta = blk == n_blocks - 1
    row = lax.broadcasted_iota(jnp.int32, log_f.shape, 0)
    log_f = jnp.where(jnp.logical_and(is_meta, row >= N_META), 0.0, log_f)
    tri = tri_ref[...]
    cs = sum(jnp.dot(tri, piece, preferred_element_type=F32) for piece in _split3(log_f))
    total = cs[META_BLOCK - 1:META_BLOCK, :]
    carry = carry_ref[...]
    bias = -LOG2E * jnp.where(is_meta, cs - total, cs + carry)
    carry_ref[...] = carry + total

    lanes = sum(jnp.dot(piece, spread_ref[p], preferred_element_type=F32)
                for p, piece in enumerate(_split3(bias)))
    lane = lax.broadcasted_iota(jnp.int32, (META_BLOCK, AUG), 1)
    row = lax.broadcasted_iota(jnp.int32, (META_BLOCK, AUG), 0)
    invalid = jnp.logical_and(is_meta, row >= N_META)
    marker = jnp.where(lane == INVALID_LANE, NEG_INF, 0.0)
    aug_ref[...] = jnp.where(invalid, marker, lanes).astype(BF16)


def _forget_bias_lanes(n, w_in, f_col, b_forget):
    rows, d = n.shape
    hf = w_in.shape[1] - f_col
    assert 3 * hf <= INVALID_LANE and f_col % AUG == 0 and hf <= AUG
    n_blocks = rows // META_BLOCK
    tri = np.tril(np.ones((META_BLOCK, META_BLOCK), np.float32))
    lane = np.arange(AUG)[None, None, :]
    spread = lane == 3 * np.arange(hf)[None, :, None] + np.arange(3)[:, None, None]
    return pl.pallas_call(
        functools.partial(_forget_kernel, n_blocks=n_blocks),
        grid=(n_blocks,),
        in_specs=[pl.BlockSpec((META_BLOCK, d), lambda b: (b, 0)),
                  pl.BlockSpec((d, AUG), lambda b: (0, f_col // AUG)),
                  pl.BlockSpec((1, hf), lambda b: (0, 0)),
                  pl.BlockSpec((META_BLOCK, META_BLOCK), lambda b: (0, 0)),
                  pl.BlockSpec((3, hf, AUG), lambda b: (0, 0, 0))],
        out_specs=pl.BlockSpec((META_BLOCK, AUG), lambda b: (b, 0)),
        out_shape=jax.ShapeDtypeStruct((rows, AUG), BF16),
        scratch_shapes=[pltpu.VMEM((1, hf), F32), pltpu.VMEM((d, hf), BF16)],
        compiler_params=_params("arbitrary"),
        name="forget_bias_lanes",
    )(n, w_in, b_forget.reshape(1, hf).astype(F32), jnp.asarray(tri, BF16),
      jnp.asarray(spread, BF16))


def _scores(q_cat, k_cat):
    return lax.dot_general(k_cat, q_cat, NT_DIMS, preferred_element_type=F32)


def _softmax_update(s, vt_blk, mask, m_ref, l_ref, acc_ref):
    if mask is not None:
        s = jnp.where(mask, s, NEG_INF)
    m_prev = m_ref[...]
    m_new = jnp.maximum(m_prev, jnp.max(s, axis=0, keepdims=True))
    alpha = jnp.exp2(m_prev - m_new)
    p = jnp.exp2(s - m_new)
    l_ref[...] = alpha * l_ref[...] + jnp.sum(p, axis=0, keepdims=True)
    acc_ref[...] = alpha * acc_ref[...] + jnp.dot(vt_blk, p.astype(BF16),
                                                  preferred_element_type=F32)
    m_ref[...] = m_new


def _causal_sweep(score, consume, sa_ref, sb_ref, qi, tq, seq):
    def start(j):
        return pl.multiple_of(j * tq, tq)

    score(seq, META_BLOCK, sb_ref)
    score(start(0), tq, sa_ref)
    consume(sb_ref, seq, META_BLOCK, None)

    def pair(p, carry):
        j = 2 * p
        score(start(j + 1), tq, sb_ref)
        consume(sa_ref, start(j), tq, None)
        score(start(j + 2), tq, sa_ref)
        consume(sb_ref, start(j + 1), tq, None)
        return carry

    lax.fori_loop(0, qi // 2, pair, 0)
    diag_mask = (lax.broadcasted_iota(jnp.int32, (tq, tq), 0)
                 <= lax.broadcasted_iota(jnp.int32, (tq, tq), 1))
    odd = qi % 2 == 1

    @pl.when(odd)
    def _():
        score(start(qi), tq, sb_ref)
        consume(sa_ref, start(qi - 1), tq, None)
        consume(sb_ref, start(qi), tq, diag_mask)

    @pl.when(jnp.logical_not(odd))
    def _():
        consume(sa_ref, start(qi), tq, diag_mask)


def _init_state(m_ref, l_ref, acc_ref):
    m_ref[...] = jnp.full_like(m_ref, NEG_INF)
    l_ref[...] = jnp.zeros_like(l_ref)
    acc_ref[...] = jnp.zeros_like(acc_ref)


def _fox_kernel(q_ref, k_ref, aug_ref, vt_ref, o_ref, m_ref, l_ref, acc_ref, qcat_ref,
                sa_ref, sb_ref, *, tq, seq, heads):
    group = pl.program_id(0)
    qi = pl.program_id(1)
    _init_state(m_ref, l_ref, acc_ref)
    lane = lax.broadcasted_iota(jnp.int32, (tq, AUG), 1)
    for a in range(heads):
        first = 3 * (group * heads + a)
        picks = jnp.logical_or(jnp.logical_and(lane >= first, lane < first + 3),
                               lane == INVALID_LANE)
        qcat_ref[a] = jnp.concatenate([q_ref[:, a * HEAD_DIM:(a + 1) * HEAD_DIM],
                                       jnp.where(picks, 1.0, 0.0).astype(BF16)], axis=1)

    head_lanes = [slice(a * HEAD_DIM, (a + 1) * HEAD_DIM) for a in range(heads)]

    def score(k0, width, s_ref):
        aug_blk = aug_ref[pl.ds(k0, width), :]
        for a in range(heads):
            k_cat = jnp.concatenate([k_ref[pl.ds(k0, width), head_lanes[a]], aug_blk], axis=1)
            s_ref[a, :width, :] = _scores(qcat_ref[a], k_cat)

    def consume(s_ref, k0, width, mask):
        for a in range(heads):
            _softmax_update(s_ref[a, :width, :], vt_ref[head_lanes[a], pl.ds(k0, width)], mask,
                            m_ref.at[a], l_ref.at[a], acc_ref.at[a])

    _causal_sweep(score, consume, sa_ref, sb_ref, qi, tq, seq)
    for a in range(heads):
        o_t = acc_ref[a] / l_ref[a]
        o_ref[:, a * HEAD_DIM:(a + 1) * HEAD_DIM] = o_t.T.astype(o_ref.dtype)


def _fox_attention(qk, aug, vt, seq, n_heads, q_col, k_col, vt_row):
    rows = qk.shape[0]
    tq = _pick(seq, 512, 128)
    heads = math.gcd(n_heads, FOX_HEADS_PER_STEP)
    width = heads * HEAD_DIM
    qb, kb, vb = q_col // width, k_col // width, vt_row // width
    resident = pl.Buffered(1)
    return pl.pallas_call(
        functools.partial(_fox_kernel, tq=tq, seq=seq, heads=heads),
        grid=(n_heads // heads, seq // tq),
        in_specs=[pl.BlockSpec((tq, width), lambda g, i: (i, qb + g)),
                  pl.BlockSpec((rows, width), lambda g, i: (0, kb + g), pipeline_mode=resident),
                  pl.BlockSpec((rows, AUG), lambda g, i: (0, 0), pipeline_mode=resident),
                  pl.BlockSpec((width, rows), lambda g, i: (vb + g, 0), pipeline_mode=resident)],
        out_specs=pl.BlockSpec((tq, width), lambda g, i: (i, g)),
        out_shape=jax.ShapeDtypeStruct((seq, n_heads * HEAD_DIM), BF16),
        scratch_shapes=[pltpu.VMEM((heads, 1, tq), F32), pltpu.VMEM((heads, 1, tq), F32),
                        pltpu.VMEM((heads, HEAD_DIM, tq), F32),
                        pltpu.VMEM((heads, tq, HEAD_DIM + AUG), BF16),
                        pltpu.VMEM((heads, tq, tq), F32), pltpu.VMEM((heads, tq, tq), F32)],
        compiler_params=_params("arbitrary", "arbitrary"),
        name="fox_attention",
    )(qk, qk, aug, vt)


def _diff_kernel(q_ref, k_ref, pos_ref, qaug_ref, vt_ref, lq1_ref, lk1_ref, lq2_ref, lk2_ref,
                 gain_ref, o_ref, m_ref, l_ref, acc_ref, qcat_ref, sa_ref, sb_ref,
                 *, tq, seq, heads, lambda_init):
    qi = pl.program_id(1)
    _init_state(m_ref, l_ref, acc_ref)
    for a in range(heads):
        q_aug = jnp.broadcast_to(qaug_ref[a], (tq, AUG)).astype(BF16)
        for c in range(2):
            lanes = slice((2 * a + c) * HEAD_DIM, (2 * a + c + 1) * HEAD_DIM)
            qcat_ref[2 * a + c] = jnp.concatenate([q_ref[:, lanes], q_aug], axis=1)

    def score(k0, width, s_ref):
        pos_blk = pos_ref[pl.ds(k0, width), :]
        for ch in range(2 * heads):
            k_cat = jnp.concatenate(
                [k_ref[pl.ds(k0, width), ch * HEAD_DIM:(ch + 1) * HEAD_DIM], pos_blk], axis=1)
            s_ref[ch, :width, :] = _scores(qcat_ref[ch], k_cat)

    def consume(s_ref, k0, width, mask):
        for a in range(heads):
            vt_blk = vt_ref[a * DIFF_V_DIM:(a + 1) * DIFF_V_DIM, pl.ds(k0, width)]
            for ch in (2 * a, 2 * a + 1):
                _softmax_update(s_ref[ch, :width, :], vt_blk, mask,
                                m_ref.at[ch], l_ref.at[ch], acc_ref.at[ch])

    _causal_sweep(score, consume, sa_ref, sb_ref, qi, tq, seq)
    lam = (jnp.exp(jnp.sum(lq1_ref[...] * lk1_ref[...], axis=1, keepdims=True))
           - jnp.exp(jnp.sum(lq2_ref[...] * lk2_ref[...], axis=1, keepdims=True))
           + lambda_init)
    for a in range(heads):
        o = (acc_ref[2 * a] / l_ref[2 * a] - lam * (acc_ref[2 * a + 1] / l_ref[2 * a + 1])).T
        ms = jnp.mean(o * o, axis=-1, keepdims=True)
        o = o * lax.rsqrt(ms + SUBLN_EPS) * gain_ref[...]
        o_ref[:, a * DIFF_V_DIM:(a + 1) * DIFF_V_DIM] = (o * (1.0 - lambda_init)).astype(o_ref.dtype)


def _alibi_lanes(n_heads, seq, rows):
    def pieces(x):
        out, rest = [], x.astype(np.float32)
        for _ in range(3):
            piece = rest.astype(ml_dtypes.bfloat16).astype(np.float32)
            out.append(piece)
            rest = rest - piece
        return out

    slopes = (2.0 ** (-8.0 * np.arange(1, n_heads + 1, dtype=np.float32) / n_heads)
              * np.float32(LOG2E)).astype(np.float32)
    q_aug = np.zeros((n_heads, 1, AUG), np.float32)
    q_aug[:, 0, INVALID_LANE] = 1.0
    for lane_idx, piece in enumerate(pieces(slopes * POS_RADIX) + pieces(slopes)):
        q_aug[:, 0, lane_idx] = piece
    idx = np.arange(rows)
    pos = np.where(idx < seq, idx + N_META, idx - seq)
    valid = idx < seq + N_META
    k_aug = np.zeros((rows, AUG), np.float32)
    k_aug[:, INVALID_LANE] = np.where(valid, 0.0, NEG_INF)
    k_aug[:, 0:3] = np.where(valid, pos // POS_RADIX, 0)[:, None]
    k_aug[:, 3:6] = np.where(valid, pos % POS_RADIX, 0)[:, None]
    return jnp.asarray(q_aug), jnp.asarray(k_aug.astype(ml_dtypes.bfloat16))


def _diff_attention(qk, vt, seq, n_heads, q_col, k_col, vt_row, lam_vecs, gain, lambda_init):
    rows = qk.shape[0]
    tq = _pick(seq, 512, 128)
    heads = math.gcd(n_heads, DIFF_HEADS_PER_STEP)
    width = heads * DIFF_V_DIM
    qb, kb, vb = q_col // width, k_col // width, vt_row // width
    q_aug, k_aug = _alibi_lanes(n_heads, seq, rows)
    vec = pl.BlockSpec((1, HEAD_DIM), lambda g, i: (0, 0))
    resident = pl.Buffered(1)
    return pl.pallas_call(
        functools.partial(_diff_kernel, tq=tq, seq=seq, heads=heads, lambda_init=lambda_init),
        grid=(n_heads // heads, seq // tq),
        in_specs=[pl.BlockSpec((tq, width), lambda g, i: (i, qb + g)),
                  pl.BlockSpec((rows, width), lambda g, i: (0, kb + g), pipeline_mode=resident),
                  pl.BlockSpec((rows, AUG), lambda g, i: (0, 0), pipeline_mode=resident),
                  pl.BlockSpec((heads, 1, AUG), lambda g, i: (g, 0, 0)),
                  pl.BlockSpec((width, rows), lambda g, i: (vb + g, 0), pipeline_mode=resident),
                  vec, vec, vec, vec,
                  pl.BlockSpec((1, DIFF_V_DIM), lambda g, i: (0, 0))],
        out_specs=pl.BlockSpec((tq, width), lambda g, i: (i, g)),
        out_shape=jax.ShapeDtypeStruct((seq, n_heads * DIFF_V_DIM), BF16),
        scratch_shapes=[pltpu.VMEM((2 * heads, 1, tq), F32), pltpu.VMEM((2 * heads, 1, tq), F32),
                        pltpu.VMEM((2 * heads, DIFF_V_DIM, tq), F32),
                        pltpu.VMEM((2 * heads, tq, HEAD_DIM + AUG), BF16),
                        pltpu.VMEM((2 * heads, tq, tq), F32), pltpu.VMEM((2 * heads, tq, tq), F32)],
        compiler_params=_params("arbitrary", "arbitrary"),
        name="diff_attention",
    )(qk, qk, k_aug, q_aug, vt, *[v.reshape(1, HEAD_DIM).astype(F32) for v in lam_vecs],
      gain.reshape(1, DIFF_V_DIM).astype(F32))


def _out_proj_kernel(od_ref, of_ref, wd_ref, wf_ref, h_ref, o_ref):
    o_ref[...] = (h_ref[...]
                  + jnp.dot(od_ref[...], wd_ref[...].astype(BF16), preferred_element_type=F32)
                  + jnp.dot(of_ref[...], wf_ref[...].astype(BF16), preferred_element_type=F32))


def _out_proj_residual(o_d, o_f, w_out, h):
    seq, dw = o_d.shape
    fw = o_f.shape[1]
    d = w_out.shape[1]
    tm = _pick(seq, 1024, 16)
    tn = _pick(d, 512, 128)
    assert dw == fw
    return pl.pallas_call(
        _out_proj_kernel,
        grid=(seq // tm, d // tn),
        in_specs=[pl.BlockSpec((tm, dw), lambda i, j: (i, 0)),
                  pl.BlockSpec((tm, fw), lambda i, j: (i, 0)),
                  pl.BlockSpec((dw, tn), lambda i, j: (0, j)),
                  pl.BlockSpec((fw, tn), lambda i, j: (1, j)),
                  pl.BlockSpec((tm, tn), lambda i, j: (i, j))],
        out_specs=pl.BlockSpec((tm, tn), lambda i, j: (i, j)),
        out_shape=jax.ShapeDtypeStruct((seq, d), F32),
        compiler_params=_params("arbitrary", "arbitrary"),
        name="mixer_out_proj",
    )(o_d, o_f, w_out, w_out, h)


def kernel(x, meta_tokens, ffn1_norm, ffn1_w_gate, ffn1_w_up, ffn1_w_down, mix_norm, w_in, b_forget, lambda_q1, lambda_k1, lambda_q2, lambda_k2, subln_gain, w_out, ffn2_norm, ffn2_w_gate, ffn2_w_up, ffn2_w_down, final_norm):
    batch, seq, d = x.shape
    depth = ffn1_norm.shape[0]
    assert batch == 1 and depth == 1 and meta_tokens.shape[0] == N_META
    diff_width = d // 2
    fox_width = d - diff_width
    n_diff = diff_width // DIFF_V_DIM
    n_fox = fox_width // HEAD_DIM
    qkv_cols = 3 * diff_width + 3 * fox_width
    lambda_init = 0.8 - 0.6 * math.exp(-0.3 * 0)

    rows = seq + META_BLOCK
    h = jnp.concatenate(
        [x[0], meta_tokens.astype(x.dtype), jnp.zeros((META_BLOCK - N_META, d), x.dtype)], axis=0)

    h = _swiglu_residual(h, rows, ffn1_norm[0], ffn1_w_gate[0], ffn1_w_up[0], ffn1_w_down[0])

    n = _rmsnorm(h, mix_norm[0], rows, BF16)
    q_scale = HEAD_DIM ** -0.5 * LOG2E
    col_scale = np.concatenate([
        np.full((diff_width,), q_scale, np.float32), np.ones((diff_width,), np.float32),
        np.full((fox_width,), q_scale, np.float32), np.ones((fox_width,), np.float32)])
    qk = _qk_proj(n, w_in[0], jnp.asarray(col_scale.reshape(1, -1)), diff_width, fox_width)
    vt = _vt_proj(n, w_in[0], diff_width, fox_width)
    aug = _forget_bias_lanes(n, w_in[0], qkv_cols, b_forget[0])
    o_d = _diff_attention(qk, vt, seq, n_diff, 0, diff_width, 0,
                          (lambda_q1[0], lambda_k1[0], lambda_q2[0], lambda_k2[0]),
                          subln_gain[0], lambda_init)
    o_f = _fox_attention(qk, aug, vt, seq, n_fox, 2 * diff_width, 2 * diff_width + fox_width,
                         diff_width)
    h = _out_proj_residual(o_d, o_f, w_out[0], h)

    h = _swiglu_residual(h, seq, ffn2_norm[0], ffn2_w_gate[0], ffn2_w_up[0], ffn2_w_down[0])
    return _rmsnorm(h, final_norm, seq, x.dtype)[None]
```

```python
import functools
import math

import jax
import jax.numpy as jnp
import ml_dtypes
import numpy as np
from jax import lax
from jax.experimental import pallas as pl
from jax.experimental.pallas import tpu as pltpu

N_META = 16
META_BLOCK = 128
HEAD_DIM = 128
DIFF_V_DIM = 2 * HEAD_DIM
AUG = 128
INVALID_LANE = AUG - 1
POS_RADIX = 64
FOX_HEADS_PER_STEP = 4
DIFF_HEADS_PER_STEP = 2
RMS_EPS = 1e-6
SUBLN_EPS = 1e-5
NEG_INF = -1e30
LOG2E = math.log2(math.e)
VMEM_LIMIT_BYTES = 56 * 1024 * 1024

F32 = jnp.float32
BF16 = jnp.bfloat16
NT_DIMS = (((1,), (1,)), ((), ()))


def _pick(n, target, mult):
    best = None
    for d in range(mult, min(n, target) + 1, mult):
        if n % d == 0:
            best = d
    assert best is not None, (n, target, mult)
    return best


def _params(*sem):
    return pltpu.CompilerParams(dimension_semantics=sem, vmem_limit_bytes=VMEM_LIMIT_BYTES)


def _split3(x):
    hi = x.astype(BF16)
    rest = x - hi.astype(F32)
    mid = rest.astype(BF16)
    lo = (rest - mid.astype(F32)).astype(BF16)
    return hi, mid, lo


def _rmsnorm_kernel(h_ref, g_ref, o_ref, *, eps):
    x = h_ref[...]
    ms = jnp.mean(x * x, axis=-1, keepdims=True)
    o_ref[...] = (x * lax.rsqrt(ms + eps) * g_ref[...]).astype(o_ref.dtype)


def _rmsnorm(h, g, rows, out_dtype):
    d = h.shape[1]
    tm = _pick(rows, 520, 8)
    return pl.pallas_call(
        functools.partial(_rmsnorm_kernel, eps=RMS_EPS),
        grid=(rows // tm,),
        in_specs=[pl.BlockSpec((tm, d), lambda i: (i, 0)),
                  pl.BlockSpec((1, d), lambda i: (0, 0))],
        out_specs=pl.BlockSpec((tm, d), lambda i: (i, 0)),
        out_shape=jax.ShapeDtypeStruct((rows, d), out_dtype),
        compiler_params=_params("arbitrary"),
        name="rmsnorm",
    )(h, g.reshape(1, d).astype(F32))


def _gate_up_kernel(n_ref, wg_ref, wu_ref, wd_ref, o_ref, wd_bf_ref):
    n = n_ref[...]
    g = jnp.dot(n, wg_ref[...].astype(BF16), preferred_element_type=F32)
    u = jnp.dot(n, wu_ref[...].astype(BF16), preferred_element_type=F32)
    o_ref[...] = (g * jax.nn.sigmoid(g) * u).astype(o_ref.dtype)
    wd_bf_ref[...] = wd_ref[...].astype(BF16)


def _gate_up(n, wg, wu, wd):
    rows, d = n.shape
    f = wg.shape[1]
    tm = _pick(rows, 1040, 16)
    tf = _pick(f, 256, 128)
    row_tiles, col_tiles = rows // tm, f // tf
    slab = f // (row_tiles * col_tiles)
    assert slab * row_tiles * col_tiles == f and slab % 16 == 0, (f, row_tiles, col_tiles)
    return pl.pallas_call(
        _gate_up_kernel,
        grid=(row_tiles, col_tiles),
        in_specs=[pl.BlockSpec((tm, d), lambda i, j: (i, 0)),
                  pl.BlockSpec((d, tf), lambda i, j: (0, j)),
                  pl.BlockSpec((d, tf), lambda i, j: (0, j)),
                  pl.BlockSpec((slab, wd.shape[1]), lambda i, j: (i * col_tiles + j, 0))],
        out_specs=[pl.BlockSpec((tm, tf), lambda i, j: (i, j)),
                   pl.BlockSpec((slab, wd.shape[1]), lambda i, j: (i * col_tiles + j, 0))],
        out_shape=[jax.ShapeDtypeStruct((rows, f), BF16),
                   jax.ShapeDtypeStruct(wd.shape, BF16)],
        compiler_params=_params("arbitrary", "arbitrary"),
        name="ffn_gate_up",
    )(n, wg, wu, wd)


def _down_kernel(a_ref, wd_ref, h_ref, o_ref):
    o_ref[...] = h_ref[...] + 0.5 * jnp.dot(a_ref[...], wd_ref[...], preferred_element_type=F32)


def _down_residual(act, wd, h):
    rows, f = act.shape
    d = wd.shape[1]
    tm = _pick(rows, 520, 16)
    tn = _pick(d, 512, 128)
    return pl.pallas_call(
        _down_kernel,
        grid=(rows // tm, d // tn),
        in_specs=[pl.BlockSpec((tm, f), lambda i, j: (i, 0)),
                  pl.BlockSpec((f, tn), lambda i, j: (0, j)),
                  pl.BlockSpec((tm, tn), lambda i, j: (i, j))],
        out_specs=pl.BlockSpec((tm, tn), lambda i, j: (i, j)),
        out_shape=jax.ShapeDtypeStruct((rows, d), F32),
        compiler_params=_params("arbitrary", "arbitrary"),
        name="ffn_down",
    )(act, wd, h)


def _swiglu_residual(h, rows, norm_g, wg, wu, wd):
    n = _rmsnorm(h, norm_g, rows, BF16)
    act, wd_bf = _gate_up(n, wg, wu, wd)
    return _down_residual(act, wd_bf, h)


def _qk_proj_kernel(n_ref, w_ref, s_ref, o_ref):
    y = jnp.dot(n_ref[...], w_ref[...].astype(BF16), preferred_element_type=F32)
    o_ref[...] = (y * s_ref[...]).astype(o_ref.dtype)


def _qk_proj(n, w_in, col_scale, diff_width, fox_width):
    rows, d = n.shape
    cols = 2 * diff_width + 2 * fox_width
    tm = _pick(rows, 1040, 16)
    tn = _pick(math.gcd(diff_width, fox_width), 1024 if w_in.dtype == BF16 else 512, 128)
    diff_blocks = 2 * diff_width // tn
    skip = diff_width // tn
    return pl.pallas_call(
        _qk_proj_kernel,
        grid=(rows // tm, cols // tn),
        in_specs=[pl.BlockSpec((tm, d), lambda i, j: (i, 0)),
                  pl.BlockSpec((d, tn), lambda i, j: (0, j + skip * (j // diff_blocks))),
                  pl.BlockSpec((1, tn), lambda i, j: (0, j))],
        out_specs=pl.BlockSpec((tm, tn), lambda i, j: (i, j)),
        out_shape=jax.ShapeDtypeStruct((rows, cols), BF16),
        compiler_params=_params("arbitrary", "arbitrary"),
        name="mixer_qk_proj",
    )(n, w_in, col_scale)


def _vt_proj_kernel(w_ref, n_ref, o_ref, v_ref):
    v_ref[...] = jnp.dot(n_ref[...], w_ref[...], preferred_element_type=F32)
    o_ref[...] = v_ref[...].T.astype(o_ref.dtype)


def _vt_proj(n, w_in, diff_width, fox_width):
    rows, d = n.shape
    feats = diff_width + fox_width
    assert w_in.dtype == BF16
    tc = _pick(math.gcd(diff_width, fox_width), 1024, 128)
    tr = _pick(rows, 640, 128)
    diff_blocks = diff_width // tc
    first_d = 2 * diff_width // tc
    first_f = (3 * diff_width + 2 * fox_width) // tc

    def w_block(c, r):
        return 0, jnp.where(c < diff_blocks, first_d + c, first_f + c - diff_blocks)

    return pl.pallas_call(
        _vt_proj_kernel,
        grid=(feats // tc, rows // tr),
        in_specs=[pl.BlockSpec((d, tc), w_block),
                  pl.BlockSpec((tr, d), lambda c, r: (r, 0))],
        out_specs=pl.BlockSpec((tc, tr), lambda c, r: (c, r)),
        out_shape=jax.ShapeDtypeStruct((feats, rows), BF16),
        scratch_shapes=[pltpu.VMEM((tr, tc), F32)],
        compiler_params=_params("arbitrary", "arbitrary"),
        name="mixer_vt_proj",
    )(w_in, n)


def _forget_kernel(n_ref, wf_ref, b_ref, tri_ref, spread_ref, aug_ref, carry_ref, wfb_ref,
                   *, n_blocks):
    blk = pl.program_id(0)
    hf = wfb_ref.shape[1]

    @pl.when(blk == 0)
    def _():
        carry_ref[...] = jnp.zeros_like(carry_ref)
        wfb_ref[...] = wf_ref[:, :hf].astype(BF16)

    logit = jnp.dot(n_ref[...], wfb_ref[...], preferred_element_type=F32) + b_ref[...]
    log_f = jnp.minimum(logit, 0.0) - jnp.log1p(jnp.exp(-jnp.abs(logit)))
    is_meta = blk == n_blocks - 1
    row = lax.broadcasted_iota(jnp.int32, log_f.shape, 0)
    log_f = jnp.where(jnp.logical_and(is_meta, row >= N_META), 0.0, log_f)
    tri = tri_ref[...]
    cs = sum(jnp.dot(tri, piece, preferred_element_type=F32) for piece in _split3(log_f))
    total = cs[META_BLOCK - 1:META_BLOCK, :]
    carry = carry_ref[...]
    bias = -LOG2E * jnp.where(is_meta, cs - total, cs + carry)
    carry_ref[...] = carry + total

    lanes = sum(jnp.dot(piece, spread_ref[p], preferred_element_type=F32)
                for p, piece in enumerate(_split3(bias)))
    lane = lax.broadcasted_iota(jnp.int32, (META_BLOCK, AUG), 1)
    row = lax.broadcasted_iota(jnp.int32, (META_BLOCK, AUG), 0)
    invalid = jnp.logical_and(is_meta, row >= N_META)
    marker = jnp.where(lane == INVALID_LANE, NEG_INF, 0.0)
    aug_ref[...] = jnp.where(invalid, marker, lanes).astype(BF16)


def _forget_bias_lanes(n, w_in, f_col, hf, b_forget):
    rows, d = n.shape
    assert 3 * hf <= INVALID_LANE and f_col % AUG == 0 and hf <= AUG
    n_blocks = rows // META_BLOCK
    tri = np.tril(np.ones((META_BLOCK, META_BLOCK), np.float32))
    lane = np.arange(AUG)[None, None, :]
    spread = lane == 3 * np.arange(hf)[None, :, None] + np.arange(3)[:, None, None]
    return pl.pallas_call(
        functools.partial(_forget_kernel, n_blocks=n_blocks),
        grid=(n_blocks,),
        in_specs=[pl.BlockSpec((META_BLOCK, d), lambda b: (b, 0)),
                  pl.BlockSpec((d, AUG), lambda b: (0, f_col // AUG)),
                  pl.BlockSpec((1, hf), lambda b: (0, 0)),
                  pl.BlockSpec((META_BLOCK, META_BLOCK), lambda b: (0, 0)),
                  pl.BlockSpec((3, hf, AUG), lambda b: (0, 0, 0))],
        out_specs=pl.BlockSpec((META_BLOCK, AUG), lambda b: (b, 0)),
        out_shape=jax.ShapeDtypeStruct((rows, AUG), BF16),
        scratch_shapes=[pltpu.VMEM((1, hf), F32), pltpu.VMEM((d, hf), BF16)],
        compiler_params=_params("arbitrary"),
        name="forget_bias_lanes",
    )(n, w_in, b_forget.reshape(1, hf).astype(F32), jnp.asarray(tri, BF16),
      jnp.asarray(spread, BF16))


def _scores(q_cat, k_cat):
    return lax.dot_general(k_cat, q_cat, NT_DIMS, preferred_element_type=F32)


def _softmax_update(s, vt_blk, mask, m_ref, l_ref, acc_ref):
    if mask is not None:
        s = jnp.where(mask, s, NEG_INF)
    m_prev = m_ref[...]
    m_new = jnp.maximum(m_prev, jnp.max(s, axis=0, keepdims=True))
    alpha = jnp.exp2(m_prev - m_new)
    p = jnp.exp2(s - m_new)
    l_ref[...] = alpha * l_ref[...] + jnp.sum(p, axis=0, keepdims=True)
    acc_ref[...] = alpha * acc_ref[...] + jnp.dot(vt_blk, p.astype(BF16),
                                                  preferred_element_type=F32)
    m_ref[...] = m_new


def _causal_sweep(score, consume, sa_ref, sb_ref, qi, tq, seq):
    def start(j):
        return pl.multiple_of(j * tq, tq)

    score(seq, META_BLOCK, sb_ref)
    score(start(0), tq, sa_ref)
    consume(sb_ref, seq, META_BLOCK, None)

    def pair(p, carry):
        j = 2 * p
        score(start(j + 1), tq, sb_ref)
        consume(sa_ref, start(j), tq, None)
        score(start(j + 2), tq, sa_ref)
        consume(sb_ref, start(j + 1), tq, None)
        return carry

    lax.fori_loop(0, qi // 2, pair, 0)
    diag_mask = (lax.broadcasted_iota(jnp.int32, (tq, tq), 0)
                 <= lax.broadcasted_iota(jnp.int32, (tq, tq), 1))
    odd = qi % 2 == 1

    @pl.when(odd)
    def _():
        score(start(qi), tq, sb_ref)
        consume(sa_ref, start(qi - 1), tq, None)
        consume(sb_ref, start(qi), tq, diag_mask)

    @pl.when(jnp.logical_not(odd))
    def _():
        consume(sa_ref, start(qi), tq, diag_mask)


def _init_state(m_ref, l_ref, acc_ref):
    m_ref[...] = jnp.full_like(m_ref, NEG_INF)
    l_ref[...] = jnp.zeros_like(l_ref)
    acc_ref[...] = jnp.zeros_like(acc_ref)


def _fox_kernel(q_ref, k_ref, aug_ref, vt_ref, o_ref, m_ref, l_ref, acc_ref, qcat_ref,
                sa_ref, sb_ref, *, tq, seq, heads):
    group = pl.program_id(0)
    qi = pl.program_id(1)
    _init_state(m_ref, l_ref, acc_ref)
    lane = lax.broadcasted_iota(jnp.int32, (tq, AUG), 1)
    for a in range(heads):
        first = 3 * (group * heads + a)
        picks = jnp.logical_or(jnp.logical_and(lane >= first, lane < first + 3),
                               lane == INVALID_LANE)
        qcat_ref[a] = jnp.concatenate([q_ref[:, a * HEAD_DIM:(a + 1) * HEAD_DIM],
                                       jnp.where(picks, 1.0, 0.0).astype(BF16)], axis=1)

    head_lanes = [slice(a * HEAD_DIM, (a + 1) * HEAD_DIM) for a in range(heads)]

    def score(k0, width, s_ref):
        aug_blk = aug_ref[pl.ds(k0, width), :]
        for a in range(heads):
            k_cat = jnp.concatenate([k_ref[pl.ds(k0, width), head_lanes[a]], aug_blk], axis=1)
            s_ref[a, :width, :] = _scores(qcat_ref[a], k_cat)

    def consume(s_ref, k0, width, mask):
        for a in range(heads):
            _softmax_update(s_ref[a, :width, :], vt_ref[head_lanes[a], pl.ds(k0, width)], mask,
                            m_ref.at[a], l_ref.at[a], acc_ref.at[a])

    _causal_sweep(score, consume, sa_ref, sb_ref, qi, tq, seq)
    for a in range(heads):
        o_t = acc_ref[a] / l_ref[a]
        o_ref[:, a * HEAD_DIM:(a + 1) * HEAD_DIM] = o_t.T.astype(o_ref.dtype)


def _fox_attention(qk, aug, vt, seq, n_heads, q_col, k_col, vt_row):
    rows = qk.shape[0]
    tq = _pick(seq, 512, 128)
    heads = math.gcd(n_heads, FOX_HEADS_PER_STEP)
    width = heads * HEAD_DIM
    qb, kb, vb = q_col // width, k_col // width, vt_row // width
    resident = pl.Buffered(1)
    return pl.pallas_call(
        functools.partial(_fox_kernel, tq=tq, seq=seq, heads=heads),
        grid=(n_heads // heads, seq // tq),
        in_specs=[pl.BlockSpec((tq, width), lambda g, i: (i, qb + g)),
                  pl.BlockSpec((rows, width), lambda g, i: (0, kb + g), pipeline_mode=resident),
                  pl.BlockSpec((rows, AUG), lambda g, i: (0, 0), pipeline_mode=resident),
                  pl.BlockSpec((width, rows), lambda g, i: (vb + g, 0), pipeline_mode=resident)],
        out_specs=pl.BlockSpec((tq, width), lambda g, i: (i, g)),
        out_shape=jax.ShapeDtypeStruct((seq, n_heads * HEAD_DIM), BF16),
        scratch_shapes=[pltpu.VMEM((heads, 1, tq), F32), pltpu.VMEM((heads, 1, tq), F32),
                        pltpu.VMEM((heads, HEAD_DIM, tq), F32),
                        pltpu.VMEM((heads, tq, HEAD_DIM + AUG), BF16),
                        pltpu.VMEM((heads, tq, tq), F32), pltpu.VMEM((heads, tq, tq), F32)],
        compiler_params=_params("arbitrary", "arbitrary"),
        name="fox_attention",
    )(qk, qk, aug, vt)


def _diff_kernel(q_ref, k_ref, pos_ref, qaug_ref, vt_ref, lq1_ref, lk1_ref, lq2_ref, lk2_ref,
                 gain_ref, o_ref, m_ref, l_ref, acc_ref, qcat_ref, sa_ref, sb_ref,
                 *, tq, seq, heads, lambda_init):
    qi = pl.program_id(1)
    _init_state(m_ref, l_ref, acc_ref)
    for a in range(heads):
        q_aug = jnp.broadcast_to(qaug_ref[a], (tq, AUG)).astype(BF16)
        for c in range(2):
            lanes = slice((2 * a + c) * HEAD_DIM, (2 * a + c + 1) * HEAD_DIM)
            qcat_ref[2 * a + c] = jnp.concatenate([q_ref[:, lanes], q_aug], axis=1)

    def score(k0, width, s_ref):
        pos_blk = pos_ref[pl.ds(k0, width), :]
        for ch in range(2 * heads):
            k_cat = jnp.concatenate(
                [k_ref[pl.ds(k0, width), ch * HEAD_DIM:(ch + 1) * HEAD_DIM], pos_blk], axis=1)
            s_ref[ch, :width, :] = _scores(qcat_ref[ch], k_cat)

    def consume(s_ref, k0, width, mask):
        for a in range(heads):
            vt_blk = vt_ref[a * DIFF_V_DIM:(a + 1) * DIFF_V_DIM, pl.ds(k0, width)]
            for ch in (2 * a, 2 * a + 1):
                _softmax_update(s_ref[ch, :width, :], vt_blk, mask,
                                m_ref.at[ch], l_ref.at[ch], acc_ref.at[ch])

    _causal_sweep(score, consume, sa_ref, sb_ref, qi, tq, seq)
    lam = (jnp.exp(jnp.sum(lq1_ref[...] * lk1_ref[...], axis=1, keepdims=True))
           - jnp.exp(jnp.sum(lq2_ref[...] * lk2_ref[...], axis=1, keepdims=True))
           + lambda_init)
    for a in range(heads):
        o = (acc_ref[2 * a] / l_ref[2 * a] - lam * (acc_ref[2 * a + 1] / l_ref[2 * a + 1])).T
        ms = jnp.mean(o * o, axis=-1, keepdims=True)
        o = o * lax.rsqrt(ms + SUBLN_EPS) * gain_ref[...]
        o_ref[:, a * DIFF_V_DIM:(a + 1) * DIFF_V_DIM] = (o * (1.0 - lambda_init)).astype(o_ref.dtype)


def _alibi_lanes(n_heads, seq, rows):
    def pieces(x):
        out, rest = [], x.astype(np.float32)
        for _ in range(3):
            piece = rest.astype(ml_dtypes.bfloat16).astype(np.float32)
            out.append(piece)
            rest = rest - piece
        return out

    slopes = (2.0 ** (-8.0 * np.arange(1, n_heads + 1, dtype=np.float32) / n_heads)
              * np.float32(LOG2E)).astype(np.float32)
    q_aug = np.zeros((n_heads, 1, AUG), np.float32)
    q_aug[:, 0, INVALID_LANE] = 1.0
    for lane_idx, piece in enumerate(pieces(slopes * POS_RADIX) + pieces(slopes)):
        q_aug[:, 0, lane_idx] = piece
    idx = np.arange(rows)
    pos = np.where(idx < seq, idx + N_META, idx - seq)
    valid = idx < seq + N_META
    k_aug = np.zeros((rows, AUG), np.float32)
    k_aug[:, INVALID_LANE] = np.where(valid, 0.0, NEG_INF)
    k_aug[:, 0:3] = np.where(valid, pos // POS_RADIX, 0)[:, None]
    k_aug[:, 3:6] = np.where(valid, pos % POS_RADIX, 0)[:, None]
    return jnp.asarray(q_aug), jnp.asarray(k_aug.astype(ml_dtypes.bfloat16))


def _diff_attention(qk, vt, seq, n_heads, q_col, k_col, vt_row, lam_vecs, gain, lambda_init):
    rows = qk.shape[0]
    tq = _pick(seq, 512, 128)
    heads = math.gcd(n_heads, DIFF_HEADS_PER_STEP)
    width = heads * DIFF_V_DIM
    qb, kb, vb = q_col // width, k_col // width, vt_row // width
    q_aug, k_aug = _alibi_lanes(n_heads, seq, rows)
    vec = pl.BlockSpec((1, HEAD_DIM), lambda g, i: (0, 0))
    resident = pl.Buffered(1)
    return pl.pallas_call(
        functools.partial(_diff_kernel, tq=tq, seq=seq, heads=heads, lambda_init=lambda_init),
        grid=(n_heads // heads, seq // tq),
        in_specs=[pl.BlockSpec((tq, width), lambda g, i: (i, qb + g)),
                  pl.BlockSpec((rows, width), lambda g, i: (0, kb + g), pipeline_mode=resident),
                  pl.BlockSpec((rows, AUG), lambda g, i: (0, 0), pipeline_mode=resident),
                  pl.BlockSpec((heads, 1, AUG), lambda g, i: (g, 0, 0)),
                  pl.BlockSpec((width, rows), lambda g, i: (vb + g, 0), pipeline_mode=resident),
                  vec, vec, vec, vec,
                  pl.BlockSpec((1, DIFF_V_DIM), lambda g, i: (0, 0))],
        out_specs=pl.BlockSpec((tq, width), lambda g, i: (i, g)),
        out_shape=jax.ShapeDtypeStruct((seq, n_heads * DIFF_V_DIM), BF16),
        scratch_shapes=[pltpu.VMEM((2 * heads, 1, tq), F32), pltpu.VMEM((2 * heads, 1, tq), F32),
                        pltpu.VMEM((2 * heads, DIFF_V_DIM, tq), F32),
                        pltpu.VMEM((2 * heads, tq, HEAD_DIM + AUG), BF16),
                        pltpu.VMEM((2 * heads, tq, tq), F32), pltpu.VMEM((2 * heads, tq, tq), F32)],
        compiler_params=_params("arbitrary", "arbitrary"),
        name="diff_attention",
    )(qk, qk, k_aug, q_aug, vt, *[v.reshape(1, HEAD_DIM).astype(F32) for v in lam_vecs],
      gain.reshape(1, DIFF_V_DIM).astype(F32))


def _out_proj_kernel(od_ref, of_ref, wd_ref, wf_ref, h_ref, o_ref):
    o_ref[...] = (h_ref[...]
                  + jnp.dot(od_ref[...], wd_ref[...].astype(BF16), preferred_element_type=F32)
                  + jnp.dot(of_ref[...], wf_ref[...].astype(BF16), preferred_element_type=F32))


def _out_proj_residual(o_d, o_f, w_out, h):
    seq, dw = o_d.shape
    fw = o_f.shape[1]
    d = w_out.shape[1]
    tm = _pick(seq, 1024, 16)
    tn = _pick(d, 512, 128)
    assert dw == fw
    return pl.pallas_call(
        _out_proj_kernel,
        grid=(seq // tm, d // tn),
        in_specs=[pl.BlockSpec((tm, dw), lambda i, j: (i, 0)),
                  pl.BlockSpec((tm, fw), lambda i, j: (i, 0)),
                  pl.BlockSpec((dw, tn), lambda i, j: (0, j)),
                  pl.BlockSpec((fw, tn), lambda i, j: (1, j)),
                  pl.BlockSpec((tm, tn), lambda i, j: (i, j))],
        out_specs=pl.BlockSpec((tm, tn), lambda i, j: (i, j)),
        out_shape=jax.ShapeDtypeStruct((seq, d), F32),
        compiler_params=_params("arbitrary", "arbitrary"),
        name="mixer_out_proj",
    )(o_d, o_f, w_out, w_out, h)


def kernel(x, meta_tokens, ffn1_norm, ffn1_w_gate, ffn1_w_up, ffn1_w_down, mix_norm, w_in, b_forget, lambda_q1, lambda_k1, lambda_q2, lambda_k2, subln_gain, w_out, ffn2_norm, ffn2_w_gate, ffn2_w_up, ffn2_w_down, final_norm):
    batch, seq, d = x.shape
    depth = ffn1_norm.shape[0]
    assert batch == 1 and depth == 1 and meta_tokens.shape[0] == N_META
    diff_width = d // 2
    fox_width = d - diff_width
    n_diff = diff_width // DIFF_V_DIM
    n_fox = fox_width // HEAD_DIM
    qkv_cols = 3 * diff_width + 3 * fox_width
    lambda_init = 0.8 - 0.6 * math.exp(-0.3 * 0)

    rows = seq + META_BLOCK
    h = jnp.concatenate(
        [x[0], meta_tokens.astype(x.dtype), jnp.zeros((META_BLOCK - N_META, d), x.dtype)], axis=0)

    h = _swiglu_residual(h, rows, ffn1_norm[0], ffn1_w_gate[0], ffn1_w_up[0], ffn1_w_down[0])

    n = _rmsnorm(h, mix_norm[0], rows, BF16)
    q_scale = HEAD_DIM ** -0.5 * LOG2E
    col_scale = np.concatenate([
        np.full((diff_width,), q_scale, np.float32), np.ones((diff_width,), np.float32),
        np.full((fox_width,), q_scale, np.float32), np.ones((fox_width,), np.float32)])
    w_in_bf = jnp.pad(w_in[0], ((0, 0), (0, -w_in.shape[2] % AUG))).astype(BF16)
    qk = _qk_proj(n, w_in_bf, jnp.asarray(col_scale.reshape(1, -1)), diff_width, fox_width)
    vt = _vt_proj(n, w_in_bf, diff_width, fox_width)
    aug = _forget_bias_lanes(n, w_in_bf, qkv_cols, w_in.shape[2] - qkv_cols, b_forget[0])
    o_d = _diff_attention(qk, vt, seq, n_diff, 0, diff_width, 0,
                          (lambda_q1[0], lambda_k1[0], lambda_q2[0], lambda_k2[0]),
                          subln_gain[0], lambda_init)
    o_f = _fox_attention(qk, aug, vt, seq, n_fox, 2 * diff_width, 2 * diff_width + fox_width,
                         diff_width)
    h = _out_proj_residual(o_d, o_f, w_out[0], h)

    h = _swiglu_residual(h, seq, ffn2_norm[0], ffn2_w_gate[0], ffn2_w_up[0], ffn2_w_down[0])
    return _rmsnorm(h, final_norm, seq, x.dtype)[None]
```

```python
import functools
import math

import jax
import jax.numpy as jnp
import ml_dtypes
import numpy as np
from jax import lax
from jax.experimental import pallas as pl
from jax.experimental.pallas import tpu as pltpu

N_META = 16
META_BLOCK = 128
HEAD_DIM = 128
DIFF_V_DIM = 2 * HEAD_DIM
AUG = 128
INVALID_LANE = AUG - 1
POS_RADIX = 64
FOX_HEADS_PER_STEP = 4
DIFF_HEADS_PER_STEP = 2
RMS_EPS = 1e-6
SUBLN_EPS = 1e-5
NEG_INF = -1e30
LOG2E = math.log2(math.e)
VMEM_LIMIT_BYTES = 56 * 1024 * 1024

F32 = jnp.float32
BF16 = jnp.bfloat16
NT_DIMS = (((1,), (1,)), ((), ()))


def _pick(n, target, mult):
    best = None
    for d in range(mult, min(n, target) + 1, mult):
        if n % d == 0:
            best = d
    assert best is not None, (n, target, mult)
    return best


def _params(*sem):
    return pltpu.CompilerParams(dimension_semantics=sem, vmem_limit_bytes=VMEM_LIMIT_BYTES)


def _split3(x):
    hi = x.astype(BF16)
    rest = x - hi.astype(F32)
    mid = rest.astype(BF16)
    lo = (rest - mid.astype(F32)).astype(BF16)
    return hi, mid, lo


def _rmsnorm_kernel(h_ref, g_ref, o_ref, *, eps):
    x = h_ref[...]
    ms = jnp.mean(x * x, axis=-1, keepdims=True)
    o_ref[...] = (x * lax.rsqrt(ms + eps) * g_ref[...]).astype(o_ref.dtype)


def _rmsnorm(h, g, rows, out_dtype):
    d = h.shape[1]
    tm = _pick(rows, 520, 8)
    return pl.pallas_call(
        functools.partial(_rmsnorm_kernel, eps=RMS_EPS),
        grid=(rows // tm,),
        in_specs=[pl.BlockSpec((tm, d), lambda i: (i, 0)),
                  pl.BlockSpec((1, d), lambda i: (0, 0))],
        out_specs=pl.BlockSpec((tm, d), lambda i: (i, 0)),
        out_shape=jax.ShapeDtypeStruct((rows, d), out_dtype),
        compiler_params=_params("arbitrary"),
        name="rmsnorm",
    )(h, g.reshape(1, d).astype(F32))


def _gate_up_kernel(n_ref, wg_ref, wu_ref, wd_ref, o_ref, wd_bf_ref):
    n = n_ref[...]
    g = jnp.dot(n, wg_ref[...].astype(BF16), preferred_element_type=F32)
    u = jnp.dot(n, wu_ref[...].astype(BF16), preferred_element_type=F32)
    o_ref[...] = (g * jax.nn.sigmoid(g) * u).astype(o_ref.dtype)
    wd_bf_ref[...] = wd_ref[...].astype(BF16)


def _gate_up(n, wg, wu, wd):
    rows, d = n.shape
    f = wg.shape[1]
    tm = _pick(rows, 1040, 16)
    tf = _pick(f, 256, 128)
    row_tiles, col_tiles = rows // tm, f // tf
    slab = f // (row_tiles * col_tiles)
    assert slab * row_tiles * col_tiles == f and slab % 16 == 0, (f, row_tiles, col_tiles)
    return pl.pallas_call(
        _gate_up_kernel,
        grid=(row_tiles, col_tiles),
        in_specs=[pl.BlockSpec((tm, d), lambda i, j: (i, 0)),
                  pl.BlockSpec((d, tf), lambda i, j: (0, j)),
                  pl.BlockSpec((d, tf), lambda i, j: (0, j)),
                  pl.BlockSpec((slab, wd.shape[1]), lambda i, j: (i * col_tiles + j, 0))],
        out_specs=[pl.BlockSpec((tm, tf), lambda i, j: (i, j)),
                   pl.BlockSpec((slab, wd.shape[1]), lambda i, j: (i * col_tiles + j, 0))],
        out_shape=[jax.ShapeDtypeStruct((rows, f), BF16),
                   jax.ShapeDtypeStruct(wd.shape, BF16)],
        compiler_params=_params("arbitrary", "arbitrary"),
        name="ffn_gate_up",
    )(n, wg, wu, wd)


def _down_kernel(a_ref, wd_ref, h_ref, o_ref):
    o_ref[...] = h_ref[...] + 0.5 * jnp.dot(a_ref[...], wd_ref[...], preferred_element_type=F32)


def _down_residual(act, wd, h):
    rows, f = act.shape
    d = wd.shape[1]
    tm = _pick(rows, 520, 16)
    tn = _pick(d, 512, 128)
    return pl.pallas_call(
        _down_kernel,
        grid=(rows // tm, d // tn),
        in_specs=[pl.BlockSpec((tm, f), lambda i, j: (i, 0)),
                  pl.BlockSpec((f, tn), lambda i, j: (0, j)),
                  pl.BlockSpec((tm, tn), lambda i, j: (i, j))],
        out_specs=pl.BlockSpec((tm, tn), lambda i, j: (i, j)),
        out_shape=jax.ShapeDtypeStruct((rows, d), F32),
        compiler_params=_params("arbitrary", "arbitrary"),
        name="ffn_down",
    )(act, wd, h)


def _swiglu_residual(h, rows, norm_g, wg, wu, wd):
    n = _rmsnorm(h, norm_g, rows, BF16)
    act, wd_bf = _gate_up(n, wg, wu, wd)
    return _down_residual(act, wd_bf, h)


def _load_weight_block(w_t_ref, wbf_ref):
    @pl.when(pl.program_id(1) == 0)
    def _():
        wbf_ref[...] = w_t_ref[...].T.astype(BF16)


def _qk_proj_kernel(n_ref, w_t_ref, s_ref, o_ref, wbf_ref):
    _load_weight_block(w_t_ref, wbf_ref)
    y = jnp.dot(n_ref[...], wbf_ref[...], preferred_element_type=F32)
    o_ref[...] = (y * s_ref[...]).astype(o_ref.dtype)


def _qk_proj(n, w_t, col_scale, diff_width, fox_width):
    rows, d = n.shape
    cols = 2 * diff_width + 2 * fox_width
    tm = _pick(rows, 1040, 16)
    tn = _pick(math.gcd(diff_width, fox_width), 512, 128)
    diff_blocks = 2 * diff_width // tn
    skip = diff_width // tn
    return pl.pallas_call(
        _qk_proj_kernel,
        grid=(cols // tn, rows // tm),
        in_specs=[pl.BlockSpec((tm, d), lambda j, i: (i, 0)),
                  pl.BlockSpec((tn, d), lambda j, i: (j + skip * (j // diff_blocks), 0)),
                  pl.BlockSpec((1, tn), lambda j, i: (0, j))],
        out_specs=pl.BlockSpec((tm, tn), lambda j, i: (i, j)),
        out_shape=jax.ShapeDtypeStruct((rows, cols), BF16),
        scratch_shapes=[pltpu.VMEM((d, tn), BF16)],
        compiler_params=_params("arbitrary", "arbitrary"),
        name="mixer_qk_proj",
    )(n, w_t, col_scale)


def _vt_proj_kernel(n_ref, w_t_ref, o_ref, wbf_ref, v_ref):
    _load_weight_block(w_t_ref, wbf_ref)
    v_ref[...] = jnp.dot(n_ref[...], wbf_ref[...], preferred_element_type=F32)
    o_ref[...] = v_ref[...].T.astype(o_ref.dtype)


def _vt_proj(n, w_t, diff_width, fox_width):
    rows, d = n.shape
    feats = diff_width + fox_width
    tc = _pick(math.gcd(diff_width, fox_width), 512, 128)
    tr = _pick(rows, 640, 128)
    diff_blocks = diff_width // tc
    first_d = 2 * diff_width // tc
    first_f = (3 * diff_width + 2 * fox_width) // tc

    def w_block(c, r):
        return jnp.where(c < diff_blocks, first_d + c, first_f + c - diff_blocks), 0

    return pl.pallas_call(
        _vt_proj_kernel,
        grid=(feats // tc, rows // tr),
        in_specs=[pl.BlockSpec((tr, d), lambda c, r: (r, 0)),
                  pl.BlockSpec((tc, d), w_block)],
        out_specs=pl.BlockSpec((tc, tr), lambda c, r: (c, r)),
        out_shape=jax.ShapeDtypeStruct((feats, rows), BF16),
        scratch_shapes=[pltpu.VMEM((d, tc), BF16), pltpu.VMEM((tr, tc), F32)],
        compiler_params=_params("arbitrary", "arbitrary"),
        name="mixer_vt_proj",
    )(n, w_t)


def _forget_kernel(n_ref, wf_ref, b_ref, tri_ref, spread_ref, aug_ref, carry_ref, wfb_ref,
                   *, n_blocks):
    blk = pl.program_id(0)

    @pl.when(blk == 0)
    def _():
        carry_ref[...] = jnp.zeros_like(carry_ref)
        wfb_ref[...] = wf_ref[:wfb_ref.shape[0], :].astype(BF16)

    logit = lax.dot_general(n_ref[...], wfb_ref[...], NT_DIMS,
                            preferred_element_type=F32) + b_ref[...]
    log_f = jnp.minimum(logit, 0.0) - jnp.log1p(jnp.exp(-jnp.abs(logit)))
    is_meta = blk == n_blocks - 1
    row = lax.broadcasted_iota(jnp.int32, log_f.shape, 0)
    log_f = jnp.where(jnp.logical_and(is_meta, row >= N_META), 0.0, log_f)
    tri = tri_ref[...]
    cs = sum(jnp.dot(tri, piece, preferred_element_type=F32) for piece in _split3(log_f))
    total = cs[META_BLOCK - 1:META_BLOCK, :]
    carry = carry_ref[...]
    bias = -LOG2E * jnp.where(is_meta, cs - total, cs + carry)
    carry_ref[...] = carry + total

    lanes = sum(jnp.dot(piece, spread_ref[p], preferred_element_type=F32)
                for p, piece in enumerate(_split3(bias)))
    lane = lax.broadcasted_iota(jnp.int32, (META_BLOCK, AUG), 1)
    row = lax.broadcasted_iota(jnp.int32, (META_BLOCK, AUG), 0)
    invalid = jnp.logical_and(is_meta, row >= N_META)
    marker = jnp.where(lane == INVALID_LANE, NEG_INF, 0.0)
    aug_ref[...] = jnp.where(invalid, marker, lanes).astype(BF16)


def _forget_bias_lanes(n, w_t, f_row, b_forget):
    rows, d = n.shape
    hf = w_t.shape[0] - f_row
    hb = -(-hf // 8) * 8
    assert 3 * hf <= INVALID_LANE and f_row % hb == 0
    n_blocks = rows // META_BLOCK
    tri = np.tril(np.ones((META_BLOCK, META_BLOCK), np.float32))
    lane = np.arange(AUG)[None, None, :]
    spread = lane == 3 * np.arange(hf)[None, :, None] + np.arange(3)[:, None, None]
    return pl.pallas_call(
        functools.partial(_forget_kernel, n_blocks=n_blocks),
        grid=(n_blocks,),
        in_specs=[pl.BlockSpec((META_BLOCK, d), lambda b: (b, 0)),
                  pl.BlockSpec((hb, d), lambda b: (f_row // hb, 0)),
                  pl.BlockSpec((1, hf), lambda b: (0, 0)),
                  pl.BlockSpec((META_BLOCK, META_BLOCK), lambda b: (0, 0)),
                  pl.BlockSpec((3, hf, AUG), lambda b: (0, 0, 0))],
        out_specs=pl.BlockSpec((META_BLOCK, AUG), lambda b: (b, 0)),
        out_shape=jax.ShapeDtypeStruct((rows, AUG), BF16),
        scratch_shapes=[pltpu.VMEM((1, hf), F32), pltpu.VMEM((hf, d), BF16)],
        compiler_params=_params("arbitrary"),
        name="forget_bias_lanes",
    )(n, w_t, b_forget.reshape(1, hf).astype(F32), jnp.asarray(tri, BF16),
      jnp.asarray(spread, BF16))


def _scores(q_cat, k_cat):
    return lax.dot_general(k_cat, q_cat, NT_DIMS, preferred_element_type=F32)


def _softmax_update(s, vt_blk, mask, m_ref, l_ref, acc_ref):
    if mask is not None:
        s = jnp.where(mask, s, NEG_INF)
    m_prev = m_ref[...]
    m_new = jnp.maximum(m_prev, jnp.max(s, axis=0, keepdims=True))
    alpha = jnp.exp2(m_prev - m_new)
    p = jnp.exp2(s - m_new)
    l_ref[...] = alpha * l_ref[...] + jnp.sum(p, axis=0, keepdims=True)
    acc_ref[...] = alpha * acc_ref[...] + jnp.dot(vt_blk, p.astype(BF16),
                                                  preferred_element_type=F32)
    m_ref[...] = m_new


def _causal_sweep(score, consume, sa_ref, sb_ref, qi, tq, seq):
    def start(j):
        return pl.multiple_of(j * tq, tq)

    score(seq, META_BLOCK, sb_ref)
    score(start(0), tq, sa_ref)
    consume(sb_ref, seq, META_BLOCK, None)

    def pair(p, carry):
        j = 2 * p
        score(start(j + 1), tq, sb_ref)
        consume(sa_ref, start(j), tq, None)
        score(start(j + 2), tq, sa_ref)
        consume(sb_ref, start(j + 1), tq, None)
        return carry

    lax.fori_loop(0, qi // 2, pair, 0)
    diag_mask = (lax.broadcasted_iota(jnp.int32, (tq, tq), 0)
                 <= lax.broadcasted_iota(jnp.int32, (tq, tq), 1))
    odd = qi % 2 == 1

    @pl.when(odd)
    def _():
        score(start(qi), tq, sb_ref)
        consume(sa_ref, start(qi - 1), tq, None)
        consume(sb_ref, start(qi), tq, diag_mask)

    @pl.when(jnp.logical_not(odd))
    def _():
        consume(sa_ref, start(qi), tq, diag_mask)


def _init_state(m_ref, l_ref, acc_ref):
    m_ref[...] = jnp.full_like(m_ref, NEG_INF)
    l_ref[...] = jnp.zeros_like(l_ref)
    acc_ref[...] = jnp.zeros_like(acc_ref)


def _fox_kernel(q_ref, k_ref, aug_ref, vt_ref, o_ref, m_ref, l_ref, acc_ref, qcat_ref,
                sa_ref, sb_ref, *, tq, seq, heads):
    group = pl.program_id(0)
    qi = pl.program_id(1)
    _init_state(m_ref, l_ref, acc_ref)
    lane = lax.broadcasted_iota(jnp.int32, (tq, AUG), 1)
    for a in range(heads):
        first = 3 * (group * heads + a)
        picks = jnp.logical_or(jnp.logical_and(lane >= first, lane < first + 3),
                               lane == INVALID_LANE)
        qcat_ref[a] = jnp.concatenate([q_ref[:, a * HEAD_DIM:(a + 1) * HEAD_DIM],
                                       jnp.where(picks, 1.0, 0.0).astype(BF16)], axis=1)

    head_lanes = [slice(a * HEAD_DIM, (a + 1) * HEAD_DIM) for a in range(heads)]

    def score(k0, width, s_ref):
        aug_blk = aug_ref[pl.ds(k0, width), :]
        for a in range(heads):
            k_cat = jnp.concatenate([k_ref[pl.ds(k0, width), head_lanes[a]], aug_blk], axis=1)
            s_ref[a, :width, :] = _scores(qcat_ref[a], k_cat)

    def consume(s_ref, k0, width, mask):
        for a in range(heads):
            _softmax_update(s_ref[a, :width, :], vt_ref[head_lanes[a], pl.ds(k0, width)], mask,
                            m_ref.at[a], l_ref.at[a], acc_ref.at[a])

    _causal_sweep(score, consume, sa_ref, sb_ref, qi, tq, seq)
    for a in range(heads):
        o_t = acc_ref[a] / l_ref[a]
        o_ref[:, a * HEAD_DIM:(a + 1) * HEAD_DIM] = o_t.T.astype(o_ref.dtype)


def _fox_attention(qk, aug, vt, seq, n_heads, q_col, k_col, vt_row):
    rows = qk.shape[0]
    tq = _pick(seq, 512, 128)
    heads = math.gcd(n_heads, FOX_HEADS_PER_STEP)
    width = heads * HEAD_DIM
    qb, kb, vb = q_col // width, k_col // width, vt_row // width
    resident = pl.Buffered(1)
    return pl.pallas_call(
        functools.partial(_fox_kernel, tq=tq, seq=seq, heads=heads),
        grid=(n_heads // heads, seq // tq),
        in_specs=[pl.BlockSpec((tq, width), lambda g, i: (i, qb + g)),
                  pl.BlockSpec((rows, width), lambda g, i: (0, kb + g), pipeline_mode=resident),
                  pl.BlockSpec((rows, AUG), lambda g, i: (0, 0), pipeline_mode=resident),
                  pl.BlockSpec((width, rows), lambda g, i: (vb + g, 0), pipeline_mode=resident)],
        out_specs=pl.BlockSpec((tq, width), lambda g, i: (i, g)),
        out_shape=jax.ShapeDtypeStruct((seq, n_heads * HEAD_DIM), BF16),
        scratch_shapes=[pltpu.VMEM((heads, 1, tq), F32), pltpu.VMEM((heads, 1, tq), F32),
                        pltpu.VMEM((heads, HEAD_DIM, tq), F32),
                        pltpu.VMEM((heads, tq, HEAD_DIM + AUG), BF16),
                        pltpu.VMEM((heads, tq, tq), F32), pltpu.VMEM((heads, tq, tq), F32)],
        compiler_params=_params("arbitrary", "arbitrary"),
        name="fox_attention",
    )(qk, qk, aug, vt)


def _diff_kernel(q_ref, k_ref, pos_ref, qaug_ref, vt_ref, lq1_ref, lk1_ref, lq2_ref, lk2_ref,
                 gain_ref, o_ref, m_ref, l_ref, acc_ref, qcat_ref, sa_ref, sb_ref,
                 *, tq, seq, heads, lambda_init):
    qi = pl.program_id(1)
    _init_state(m_ref, l_ref, acc_ref)
    for a in range(heads):
        q_aug = jnp.broadcast_to(qaug_ref[a], (tq, AUG)).astype(BF16)
        for c in range(2):
            lanes = slice((2 * a + c) * HEAD_DIM, (2 * a + c + 1) * HEAD_DIM)
            qcat_ref[2 * a + c] = jnp.concatenate([q_ref[:, lanes], q_aug], axis=1)

    def score(k0, width, s_ref):
        pos_blk = pos_ref[pl.ds(k0, width), :]
        for ch in range(2 * heads):
            k_cat = jnp.concatenate(
                [k_ref[pl.ds(k0, width), ch * HEAD_DIM:(ch + 1) * HEAD_DIM], pos_blk], axis=1)
            s_ref[ch, :width, :] = _scores(qcat_ref[ch], k_cat)

    def consume(s_ref, k0, width, mask):
        for a in range(heads):
            vt_blk = vt_ref[a * DIFF_V_DIM:(a + 1) * DIFF_V_DIM, pl.ds(k0, width)]
            for ch in (2 * a, 2 * a + 1):
                _softmax_update(s_ref[ch, :width, :], vt_blk, mask,
                                m_ref.at[ch], l_ref.at[ch], acc_ref.at[ch])

    _causal_sweep(score, consume, sa_ref, sb_ref, qi, tq, seq)
    lam = (jnp.exp(jnp.sum(lq1_ref[...] * lk1_ref[...], axis=1, keepdims=True))
           - jnp.exp(jnp.sum(lq2_ref[...] * lk2_ref[...], axis=1, keepdims=True))
           + lambda_init)
    for a in range(heads):
        o = (acc_ref[2 * a] / l_ref[2 * a] - lam * (acc_ref[2 * a + 1] / l_ref[2 * a + 1])).T
        ms = jnp.mean(o * o, axis=-1, keepdims=True)
        o = o * lax.rsqrt(ms + SUBLN_EPS) * gain_ref[...]
        o_ref[:, a * DIFF_V_DIM:(a + 1) * DIFF_V_DIM] = (o * (1.0 - lambda_init)).astype(o_ref.dtype)


def _alibi_lanes(n_heads, seq, rows):
    def pieces(x):
        out, rest = [], x.astype(np.float32)
        for _ in range(3):
            piece = rest.astype(ml_dtypes.bfloat16).astype(np.float32)
            out.append(piece)
            rest = rest - piece
        return out

    slopes = (2.0 ** (-8.0 * np.arange(1, n_heads + 1, dtype=np.float32) / n_heads)
              * np.float32(LOG2E)).astype(np.float32)
    q_aug = np.zeros((n_heads, 1, AUG), np.float32)
    q_aug[:, 0, INVALID_LANE] = 1.0
    for lane_idx, piece in enumerate(pieces(slopes * POS_RADIX) + pieces(slopes)):
        q_aug[:, 0, lane_idx] = piece
    idx = np.arange(rows)
    pos = np.where(idx < seq, idx + N_META, idx - seq)
    valid = idx < seq + N_META
    k_aug = np.zeros((rows, AUG), np.float32)
    k_aug[:, INVALID_LANE] = np.where(valid, 0.0, NEG_INF)
    k_aug[:, 0:3] = np.where(valid, pos // POS_RADIX, 0)[:, None]
    k_aug[:, 3:6] = np.where(valid, pos % POS_RADIX, 0)[:, None]
    return jnp.asarray(q_aug), jnp.asarray(k_aug.astype(ml_dtypes.bfloat16))


def _diff_attention(qk, vt, seq, n_heads, q_col, k_col, vt_row, lam_vecs, gain, lambda_init):
    rows = qk.shape[0]
    tq = _pick(seq, 512, 128)
    heads = math.gcd(n_heads, DIFF_HEADS_PER_STEP)
    width = heads * DIFF_V_DIM
    qb, kb, vb = q_col // width, k_col // width, vt_row // width
    q_aug, k_aug = _alibi_lanes(n_heads, seq, rows)
    vec = pl.BlockSpec((1, HEAD_DIM), lambda g, i: (0, 0))
    resident = pl.Buffered(1)
    return pl.pallas_call(
        functools.partial(_diff_kernel, tq=tq, seq=seq, heads=heads, lambda_init=lambda_init),
        grid=(n_heads // heads, seq // tq),
        in_specs=[pl.BlockSpec((tq, width), lambda g, i: (i, qb + g)),
                  pl.BlockSpec((rows, width), lambda g, i: (0, kb + g), pipeline_mode=resident),
                  pl.BlockSpec((rows, AUG), lambda g, i: (0, 0), pipeline_mode=resident),
                  pl.BlockSpec((heads, 1, AUG), lambda g, i: (g, 0, 0)),
                  pl.BlockSpec((width, rows), lambda g, i: (vb + g, 0), pipeline_mode=resident),
                  vec, vec, vec, vec,
                  pl.BlockSpec((1, DIFF_V_DIM), lambda g, i: (0, 0))],
        out_specs=pl.BlockSpec((tq, width), lambda g, i: (i, g)),
        out_shape=jax.ShapeDtypeStruct((seq, n_heads * DIFF_V_DIM), BF16),
        scratch_shapes=[pltpu.VMEM((2 * heads, 1, tq), F32), pltpu.VMEM((2 * heads, 1, tq), F32),
                        pltpu.VMEM((2 * heads, DIFF_V_DIM, tq), F32),
                        pltpu.VMEM((2 * heads, tq, HEAD_DIM + AUG), BF16),
                        pltpu.VMEM((2 * heads, tq, tq), F32), pltpu.VMEM((2 * heads, tq, tq), F32)],
        compiler_params=_params("arbitrary", "arbitrary"),
        name="diff_attention",
    )(qk, qk, k_aug, q_aug, vt, *[v.reshape(1, HEAD_DIM).astype(F32) for v in lam_vecs],
      gain.reshape(1, DIFF_V_DIM).astype(F32))


def _out_proj_kernel(od_ref, of_ref, wd_ref, wf_ref, h_ref, o_ref):
    o_ref[...] = (h_ref[...]
                  + jnp.dot(od_ref[...], wd_ref[...].astype(BF16), preferred_element_type=F32)
                  + jnp.dot(of_ref[...], wf_ref[...].astype(BF16), preferred_element_type=F32))


def _out_proj_residual(o_d, o_f, w_out, h):
    seq, dw = o_d.shape
    fw = o_f.shape[1]
    d = w_out.shape[1]
    tm = _pick(seq, 1024, 16)
    tn = _pick(d, 512, 128)
    assert dw == fw
    return pl.pallas_call(
        _out_proj_kernel,
        grid=(seq // tm, d // tn),
        in_specs=[pl.BlockSpec((tm, dw), lambda i, j: (i, 0)),
                  pl.BlockSpec((tm, fw), lambda i, j: (i, 0)),
                  pl.BlockSpec((dw, tn), lambda i, j: (0, j)),
                  pl.BlockSpec((fw, tn), lambda i, j: (1, j)),
                  pl.BlockSpec((tm, tn), lambda i, j: (i, j))],
        out_specs=pl.BlockSpec((tm, tn), lambda i, j: (i, j)),
        out_shape=jax.ShapeDtypeStruct((seq, d), F32),
        compiler_params=_params("arbitrary", "arbitrary"),
        name="mixer_out_proj",
    )(o_d, o_f, w_out, w_out, h)


def kernel(x, meta_tokens, ffn1_norm, ffn1_w_gate, ffn1_w_up, ffn1_w_down, mix_norm, w_in, b_forget, lambda_q1, lambda_k1, lambda_q2, lambda_k2, subln_gain, w_out, ffn2_norm, ffn2_w_gate, ffn2_w_up, ffn2_w_down, final_norm):
    batch, seq, d = x.shape
    depth = ffn1_norm.shape[0]
    assert batch == 1 and depth == 1 and meta_tokens.shape[0] == N_META
    diff_width = d // 2
    fox_width = d - diff_width
    n_diff = diff_width // DIFF_V_DIM
    n_fox = fox_width // HEAD_DIM
    qkv_cols = 3 * diff_width + 3 * fox_width
    lambda_init = 0.8 - 0.6 * math.exp(-0.3 * 0)

    rows = seq + META_BLOCK
    h = jnp.concatenate(
        [x[0], meta_tokens.astype(x.dtype), jnp.zeros((META_BLOCK - N_META, d), x.dtype)], axis=0)

    h = _swiglu_residual(h, rows, ffn1_norm[0], ffn1_w_gate[0], ffn1_w_up[0], ffn1_w_down[0])

    n = _rmsnorm(h, mix_norm[0], rows, BF16)
    q_scale = HEAD_DIM ** -0.5 * LOG2E
    col_scale = np.concatenate([
        np.full((diff_width,), q_scale, np.float32), np.ones((diff_width,), np.float32),
        np.full((fox_width,), q_scale, np.float32), np.ones((fox_width,), np.float32)])
    w_t = w_in[0].T
    qk = _qk_proj(n, w_t, jnp.asarray(col_scale.reshape(1, -1)), diff_width, fox_width)
    vt = _vt_proj(n, w_t, diff_width, fox_width)
    aug = _forget_bias_lanes(n, w_t, qkv_cols, b_forget[0])
    o_d = _diff_attention(qk, vt, seq, n_diff, 0, diff_width, 0,
                          (lambda_q1[0], lambda_k1[0], lambda_q2[0], lambda_k2[0]),
                          subln_gain[0], lambda_init)
    o_f = _fox_attention(qk, aug, vt, seq, n_fox, 2 * diff_width, 2 * diff_width + fox_width,
                         diff_width)
    h = _out_proj_residual(o_d, o_f, w_out[0], h)

    h = _swiglu_residual(h, seq, ffn2_norm[0], ffn2_w_gate[0], ffn2_w_up[0], ffn2_w_down[0])
    return _rmsnorm(h, final_norm, seq, x.dtype)[None]
```

```python
import functools
import math

import jax
import jax.numpy as jnp
import ml_dtypes
import numpy as np
from jax import lax
from jax.experimental import pallas as pl
from jax.experimental.pallas import tpu as pltpu

N_META = 16
META_BLOCK = 128
HEAD_DIM = 128
DIFF_V_DIM = 2 * HEAD_DIM
AUG = 128
INVALID_LANE = AUG - 1
POS_RADIX = 64
FOX_HEADS_PER_STEP = 4
DIFF_HEADS_PER_STEP = 2
RMS_EPS = 1e-6
SUBLN_EPS = 1e-5
NEG_INF = -1e30
LOG2E = math.log2(math.e)
VMEM_LIMIT_BYTES = 56 * 1024 * 1024

F32 = jnp.float32
BF16 = jnp.bfloat16
NT_DIMS = (((1,), (1,)), ((), ()))


def _pick(n, target, mult):
    best = None
    for d in range(mult, min(n, target) + 1, mult):
        if n % d == 0:
            best = d
    assert best is not None, (n, target, mult)
    return best


def _params(*sem):
    return pltpu.CompilerParams(dimension_semantics=sem, vmem_limit_bytes=VMEM_LIMIT_BYTES)


def _split3(x):
    hi = x.astype(BF16)
    rest = x - hi.astype(F32)
    mid = rest.astype(BF16)
    lo = (rest - mid.astype(F32)).astype(BF16)
    return hi, mid, lo


def _rmsnorm_kernel(h_ref, g_ref, o_ref, *, eps):
    x = h_ref[...]
    ms = jnp.mean(x * x, axis=-1, keepdims=True)
    o_ref[...] = (x * lax.rsqrt(ms + eps) * g_ref[...]).astype(o_ref.dtype)


def _rmsnorm(h, g, rows, out_dtype):
    d = h.shape[1]
    tm = _pick(rows, 520, 8)
    return pl.pallas_call(
        functools.partial(_rmsnorm_kernel, eps=RMS_EPS),
        grid=(rows // tm,),
        in_specs=[pl.BlockSpec((tm, d), lambda i: (i, 0)),
                  pl.BlockSpec((1, d), lambda i: (0, 0))],
        out_specs=pl.BlockSpec((tm, d), lambda i: (i, 0)),
        out_shape=jax.ShapeDtypeStruct((rows, d), out_dtype),
        compiler_params=_params("arbitrary"),
        name="rmsnorm",
    )(h, g.reshape(1, d).astype(F32))


def _gate_up_kernel(n_ref, wg_ref, wu_ref, wd_ref, o_ref, wd_bf_ref):
    n = n_ref[...]
    g = jnp.dot(n, wg_ref[...].astype(BF16), preferred_element_type=F32)
    u = jnp.dot(n, wu_ref[...].astype(BF16), preferred_element_type=F32)
    o_ref[...] = (g * jax.nn.sigmoid(g) * u).astype(o_ref.dtype)
    wd_bf_ref[...] = wd_ref[...].astype(BF16)


def _gate_up(n, wg, wu, wd):
    rows, d = n.shape
    f = wg.shape[1]
    tm = _pick(rows, 1040, 16)
    tf = _pick(f, 256, 128)
    row_tiles, col_tiles = rows // tm, f // tf
    slab = f // (row_tiles * col_tiles)
    assert slab * row_tiles * col_tiles == f and slab % 16 == 0, (f, row_tiles, col_tiles)
    return pl.pallas_call(
        _gate_up_kernel,
        grid=(row_tiles, col_tiles),
        in_specs=[pl.BlockSpec((tm, d), lambda i, j: (i, 0)),
                  pl.BlockSpec((d, tf), lambda i, j: (0, j)),
                  pl.BlockSpec((d, tf), lambda i, j: (0, j)),
                  pl.BlockSpec((slab, wd.shape[1]), lambda i, j: (i * col_tiles + j, 0))],
        out_specs=[pl.BlockSpec((tm, tf), lambda i, j: (i, j)),
                   pl.BlockSpec((slab, wd.shape[1]), lambda i, j: (i * col_tiles + j, 0))],
        out_shape=[jax.ShapeDtypeStruct((rows, f), BF16),
                   jax.ShapeDtypeStruct(wd.shape, BF16)],
        compiler_params=_params("arbitrary", "arbitrary"),
        name="ffn_gate_up",
    )(n, wg, wu, wd)


def _down_kernel(a_ref, wd_ref, h_ref, o_ref):
    o_ref[...] = h_ref[...] + 0.5 * jnp.dot(a_ref[...], wd_ref[...], preferred_element_type=F32)


def _down_residual(act, wd, h):
    rows, f = act.shape
    d = wd.shape[1]
    tm = _pick(rows, 520, 16)
    tn = _pick(d, 512, 128)
    return pl.pallas_call(
        _down_kernel,
        grid=(d // tn, rows // tm),
        in_specs=[pl.BlockSpec((tm, f), lambda j, i: (i, 0)),
                  pl.BlockSpec((f, tn), lambda j, i: (0, j)),
                  pl.BlockSpec((tm, tn), lambda j, i: (i, j))],
        out_specs=pl.BlockSpec((tm, tn), lambda j, i: (i, j)),
        out_shape=jax.ShapeDtypeStruct((rows, d), F32),
        compiler_params=_params("arbitrary", "arbitrary"),
        name="ffn_down",
    )(act, wd, h)


def _swiglu_residual(h, rows, norm_g, wg, wu, wd):
    n = _rmsnorm(h, norm_g, rows, BF16)
    act, wd_bf = _gate_up(n, wg, wu, wd)
    return _down_residual(act, wd_bf, h)


def _load_weight_block(w_t_ref, wbf_ref):
    @pl.when(pl.program_id(1) == 0)
    def _():
        wbf_ref[...] = w_t_ref[...].T.astype(BF16)


def _qk_proj_kernel(n_ref, w_t_ref, s_ref, o_ref, wbf_ref):
    _load_weight_block(w_t_ref, wbf_ref)
    y = jnp.dot(n_ref[...], wbf_ref[...], preferred_element_type=F32)
    o_ref[...] = (y * s_ref[...]).astype(o_ref.dtype)


def _qk_proj(n, w_t, col_scale, diff_width, fox_width):
    rows, d = n.shape
    cols = 2 * diff_width + 2 * fox_width
    tm = _pick(rows, 1040, 16)
    tn = _pick(math.gcd(diff_width, fox_width), 512, 128)
    diff_blocks = 2 * diff_width // tn
    skip = diff_width // tn
    return pl.pallas_call(
        _qk_proj_kernel,
        grid=(cols // tn, rows // tm),
        in_specs=[pl.BlockSpec((tm, d), lambda j, i: (i, 0)),
                  pl.BlockSpec((tn, d), lambda j, i: (j + skip * (j // diff_blocks), 0)),
                  pl.BlockSpec((1, tn), lambda j, i: (0, j))],
        out_specs=pl.BlockSpec((tm, tn), lambda j, i: (i, j)),
        out_shape=jax.ShapeDtypeStruct((rows, cols), BF16),
        scratch_shapes=[pltpu.VMEM((d, tn), BF16)],
        compiler_params=_params("arbitrary", "arbitrary"),
        name="mixer_qk_proj",
    )(n, w_t, col_scale)


def _vt_proj_kernel(n_ref, w_t_ref, o_ref, wbf_ref, v_ref):
    _load_weight_block(w_t_ref, wbf_ref)
    v_ref[...] = jnp.dot(n_ref[...], wbf_ref[...], preferred_element_type=F32)
    o_ref[...] = v_ref[...].T.astype(o_ref.dtype)


def _vt_proj(n, w_t, diff_width, fox_width):
    rows, d = n.shape
    feats = diff_width + fox_width
    tc = _pick(math.gcd(diff_width, fox_width), 512, 128)
    tr = _pick(rows, 640, 128)
    diff_blocks = diff_width // tc
    first_d = 2 * diff_width // tc
    first_f = (3 * diff_width + 2 * fox_width) // tc

    def w_block(c, r):
        return jnp.where(c < diff_blocks, first_d + c, first_f + c - diff_blocks), 0

    return pl.pallas_call(
        _vt_proj_kernel,
        grid=(feats // tc, rows // tr),
        in_specs=[pl.BlockSpec((tr, d), lambda c, r: (r, 0)),
                  pl.BlockSpec((tc, d), w_block)],
        out_specs=pl.BlockSpec((tc, tr), lambda c, r: (c, r)),
        out_shape=jax.ShapeDtypeStruct((feats, rows), BF16),
        scratch_shapes=[pltpu.VMEM((d, tc), BF16), pltpu.VMEM((tr, tc), F32)],
        compiler_params=_params("arbitrary", "arbitrary"),
        name="mixer_vt_proj",
    )(n, w_t)


def _forget_kernel(n_ref, wf_ref, b_ref, tri_ref, spread_ref, aug_ref, carry_ref, wfb_ref,
                   *, n_blocks):
    blk = pl.program_id(0)

    @pl.when(blk == 0)
    def _():
        carry_ref[...] = jnp.zeros_like(carry_ref)
        wfb_ref[...] = wf_ref[:wfb_ref.shape[0], :].astype(BF16)

    logit = lax.dot_general(n_ref[...], wfb_ref[...], NT_DIMS,
                            preferred_element_type=F32) + b_ref[...]
    log_f = jnp.minimum(logit, 0.0) - jnp.log1p(jnp.exp(-jnp.abs(logit)))
    is_meta = blk == n_blocks - 1
    row = lax.broadcasted_iota(jnp.int32, log_f.shape, 0)
    log_f = jnp.where(jnp.logical_and(is_meta, row >= N_META), 0.0, log_f)
    tri = tri_ref[...]
    cs = sum(jnp.dot(tri, piece, preferred_element_type=F32) for piece in _split3(log_f))
    total = cs[META_BLOCK - 1:META_BLOCK, :]
    carry = carry_ref[...]
    bias = -LOG2E * jnp.where(is_meta, cs - total, cs + carry)
    carry_ref[...] = carry + total

    lanes = sum(jnp.dot(piece, spread_ref[p], preferred_element_type=F32)
                for p, piece in enumerate(_split3(bias)))
    lane = lax.broadcasted_iota(jnp.int32, (META_BLOCK, AUG), 1)
    row = lax.broadcasted_iota(jnp.int32, (META_BLOCK, AUG), 0)
    invalid = jnp.logical_and(is_meta, row >= N_META)
    marker = jnp.where(lane == INVALID_LANE, NEG_INF, 0.0)
    aug_ref[...] = jnp.where(invalid, marker, lanes).astype(BF16)


def _forget_bias_lanes(n, w_t, f_row, b_forget):
    rows, d = n.shape
    hf = w_t.shape[0] - f_row
    hb = -(-hf // 8) * 8
    assert 3 * hf <= INVALID_LANE and f_row % hb == 0
    n_blocks = rows // META_BLOCK
    tri = np.tril(np.ones((META_BLOCK, META_BLOCK), np.float32))
    lane = np.arange(AUG)[None, None, :]
    spread = lane == 3 * np.arange(hf)[None, :, None] + np.arange(3)[:, None, None]
    return pl.pallas_call(
        functools.partial(_forget_kernel, n_blocks=n_blocks),
        grid=(n_blocks,),
        in_specs=[pl.BlockSpec((META_BLOCK, d), lambda b: (b, 0)),
                  pl.BlockSpec((hb, d), lambda b: (f_row // hb, 0)),
                  pl.BlockSpec((1, hf), lambda b: (0, 0)),
                  pl.BlockSpec((META_BLOCK, META_BLOCK), lambda b: (0, 0)),
                  pl.BlockSpec((3, hf, AUG), lambda b: (0, 0, 0))],
        out_specs=pl.BlockSpec((META_BLOCK, AUG), lambda b: (b, 0)),
        out_shape=jax.ShapeDtypeStruct((rows, AUG), BF16),
        scratch_shapes=[pltpu.VMEM((1, hf), F32), pltpu.VMEM((hf, d), BF16)],
        compiler_params=_params("arbitrary"),
        name="forget_bias_lanes",
    )(n, w_t, b_forget.reshape(1, hf).astype(F32), jnp.asarray(tri, BF16),
      jnp.asarray(spread, BF16))


def _scores(q_cat, k_cat):
    return lax.dot_general(k_cat, q_cat, NT_DIMS, preferred_element_type=F32)


def _softmax_update(s, vt_blk, mask, m_ref, l_ref, acc_ref):
    if mask is not None:
        s = jnp.where(mask, s, NEG_INF)
    m_prev = m_ref[...]
    m_new = jnp.maximum(m_prev, jnp.max(s, axis=0, keepdims=True))
    alpha = jnp.exp2(m_prev - m_new)
    p = jnp.exp2(s - m_new)
    l_ref[...] = alpha * l_ref[...] + jnp.sum(p, axis=0, keepdims=True)
    acc_ref[...] = alpha * acc_ref[...] + jnp.dot(vt_blk, p.astype(BF16),
                                                  preferred_element_type=F32)
    m_ref[...] = m_new


def _causal_sweep(score, consume, sa_ref, sb_ref, qi, tq, seq):
    def start(j):
        return pl.multiple_of(j * tq, tq)

    score(seq, META_BLOCK, sb_ref)
    score(start(0), tq, sa_ref)
    consume(sb_ref, seq, META_BLOCK, None)

    def pair(p, carry):
        j = 2 * p
        score(start(j + 1), tq, sb_ref)
        consume(sa_ref, start(j), tq, None)
        score(start(j + 2), tq, sa_ref)
        consume(sb_ref, start(j + 1), tq, None)
        return carry

    lax.fori_loop(0, qi // 2, pair, 0)
    diag_mask = (lax.broadcasted_iota(jnp.int32, (tq, tq), 0)
                 <= lax.broadcasted_iota(jnp.int32, (tq, tq), 1))
    odd = qi % 2 == 1

    @pl.when(odd)
    def _():
        score(start(qi), tq, sb_ref)
        consume(sa_ref, start(qi - 1), tq, None)
        consume(sb_ref, start(qi), tq, diag_mask)

    @pl.when(jnp.logical_not(odd))
    def _():
        consume(sa_ref, start(qi), tq, diag_mask)


def _init_state(m_ref, l_ref, acc_ref):
    m_ref[...] = jnp.full_like(m_ref, NEG_INF)
    l_ref[...] = jnp.zeros_like(l_ref)
    acc_ref[...] = jnp.zeros_like(acc_ref)


def _fox_kernel(q_ref, k_ref, aug_ref, vt_ref, o_ref, m_ref, l_ref, acc_ref, qcat_ref,
                sa_ref, sb_ref, *, tq, seq, heads):
    group = pl.program_id(0)
    qi = pl.program_id(1)
    _init_state(m_ref, l_ref, acc_ref)
    lane = lax.broadcasted_iota(jnp.int32, (tq, AUG), 1)
    for a in range(heads):
        first = 3 * (group * heads + a)
        picks = jnp.logical_or(jnp.logical_and(lane >= first, lane < first + 3),
                               lane == INVALID_LANE)
        qcat_ref[a] = jnp.concatenate([q_ref[:, a * HEAD_DIM:(a + 1) * HEAD_DIM],
                                       jnp.where(picks, 1.0, 0.0).astype(BF16)], axis=1)

    head_lanes = [slice(a * HEAD_DIM, (a + 1) * HEAD_DIM) for a in range(heads)]

    def score(k0, width, s_ref):
        aug_blk = aug_ref[pl.ds(k0, width), :]
        for a in range(heads):
            k_cat = jnp.concatenate([k_ref[pl.ds(k0, width), head_lanes[a]], aug_blk], axis=1)
            s_ref[a, :width, :] = _scores(qcat_ref[a], k_cat)

    def consume(s_ref, k0, width, mask):
        for a in range(heads):
            _softmax_update(s_ref[a, :width, :], vt_ref[head_lanes[a], pl.ds(k0, width)], mask,
                            m_ref.at[a], l_ref.at[a], acc_ref.at[a])

    _causal_sweep(score, consume, sa_ref, sb_ref, qi, tq, seq)
    for a in range(heads):
        o_t = acc_ref[a] / l_ref[a]
        o_ref[:, a * HEAD_DIM:(a + 1) * HEAD_DIM] = o_t.T.astype(o_ref.dtype)


def _fox_attention(qk, aug, vt, seq, n_heads, q_col, k_col, vt_row):
    rows = qk.shape[0]
    tq = _pick(seq, 512, 128)
    heads = math.gcd(n_heads, FOX_HEADS_PER_STEP)
    width = heads * HEAD_DIM
    qb, kb, vb = q_col // width, k_col // width, vt_row // width
    resident = pl.Buffered(1)
    return pl.pallas_call(
        functools.partial(_fox_kernel, tq=tq, seq=seq, heads=heads),
        grid=(n_heads // heads, seq // tq),
        in_specs=[pl.BlockSpec((tq, width), lambda g, i: (i, qb + g)),
                  pl.BlockSpec((rows, width), lambda g, i: (0, kb + g), pipeline_mode=resident),
                  pl.BlockSpec((rows, AUG), lambda g, i: (0, 0), pipeline_mode=resident),
                  pl.BlockSpec((width, rows), lambda g, i: (vb + g, 0), pipeline_mode=resident)],
        out_specs=pl.BlockSpec((tq, width), lambda g, i: (i, g)),
        out_shape=jax.ShapeDtypeStruct((seq, n_heads * HEAD_DIM), BF16),
        scratch_shapes=[pltpu.VMEM((heads, 1, tq), F32), pltpu.VMEM((heads, 1, tq), F32),
                        pltpu.VMEM((heads, HEAD_DIM, tq), F32),
                        pltpu.VMEM((heads, tq, HEAD_DIM + AUG), BF16),
                        pltpu.VMEM((heads, tq, tq), F32), pltpu.VMEM((heads, tq, tq), F32)],
        compiler_params=_params("arbitrary", "arbitrary"),
        name="fox_attention",
    )(qk, qk, aug, vt)


def _diff_kernel(q_ref, k_ref, pos_ref, qaug_ref, vt_ref, lq1_ref, lk1_ref, lq2_ref, lk2_ref,
                 gain_ref, o_ref, m_ref, l_ref, acc_ref, qcat_ref, sa_ref, sb_ref,
                 *, tq, seq, heads, lambda_init):
    qi = pl.program_id(1)
    _init_state(m_ref, l_ref, acc_ref)
    for a in range(heads):
        q_aug = jnp.broadcast_to(qaug_ref[a], (tq, AUG)).astype(BF16)
        for c in range(2):
            lanes = slice((2 * a + c) * HEAD_DIM, (2 * a + c + 1) * HEAD_DIM)
            qcat_ref[2 * a + c] = jnp.concatenate([q_ref[:, lanes], q_aug], axis=1)

    def score(k0, width, s_ref):
        pos_blk = pos_ref[pl.ds(k0, width), :]
        for ch in range(2 * heads):
            k_cat = jnp.concatenate(
                [k_ref[pl.ds(k0, width), ch * HEAD_DIM:(ch + 1) * HEAD_DIM], pos_blk], axis=1)
            s_ref[ch, :width, :] = _scores(qcat_ref[ch], k_cat)

    def consume(s_ref, k0, width, mask):
        for a in range(heads):
            vt_blk = vt_ref[a * DIFF_V_DIM:(a + 1) * DIFF_V_DIM, pl.ds(k0, width)]
            for ch in (2 * a, 2 * a + 1):
                _softmax_update(s_ref[ch, :width, :], vt_blk, mask,
                                m_ref.at[ch], l_ref.at[ch], acc_ref.at[ch])

    _causal_sweep(score, consume, sa_ref, sb_ref, qi, tq, seq)
    lam = (jnp.exp(jnp.sum(lq1_ref[...] * lk1_ref[...], axis=1, keepdims=True))
           - jnp.exp(jnp.sum(lq2_ref[...] * lk2_ref[...], axis=1, keepdims=True))
           + lambda_init)
    for a in range(heads):
        o = (acc_ref[2 * a] / l_ref[2 * a] - lam * (acc_ref[2 * a + 1] / l_ref[2 * a + 1])).T
        ms = jnp.mean(o * o, axis=-1, keepdims=True)
        o = o * lax.rsqrt(ms + SUBLN_EPS) * gain_ref[...]
        o_ref[:, a * DIFF_V_DIM:(a + 1) * DIFF_V_DIM] = (o * (1.0 - lambda_init)).astype(o_ref.dtype)


def _alibi_lanes(n_heads, seq, rows):
    def pieces(x):
        out, rest = [], x.astype(np.float32)
        for _ in range(3):
            piece = rest.astype(ml_dtypes.bfloat16).astype(np.float32)
            out.append(piece)
            rest = rest - piece
        return out

    slopes = (2.0 ** (-8.0 * np.arange(1, n_heads + 1, dtype=np.float32) / n_heads)
              * np.float32(LOG2E)).astype(np.float32)
    q_aug = np.zeros((n_heads, 1, AUG), np.float32)
    q_aug[:, 0, INVALID_LANE] = 1.0
    for lane_idx, piece in enumerate(pieces(slopes * POS_RADIX) + pieces(slopes)):
        q_aug[:, 0, lane_idx] = piece
    idx = np.arange(rows)
    pos = np.where(idx < seq, idx + N_META, idx - seq)
    valid = idx < seq + N_META
    k_aug = np.zeros((rows, AUG), np.float32)
    k_aug[:, INVALID_LANE] = np.where(valid, 0.0, NEG_INF)
    k_aug[:, 0:3] = np.where(valid, pos // POS_RADIX, 0)[:, None]
    k_aug[:, 3:6] = np.where(valid, pos % POS_RADIX, 0)[:, None]
    return jnp.asarray(q_aug), jnp.asarray(k_aug.astype(ml_dtypes.bfloat16))


def _diff_attention(qk, vt, seq, n_heads, q_col, k_col, vt_row, lam_vecs, gain, lambda_init):
    rows = qk.shape[0]
    tq = _pick(seq, 512, 128)
    heads = math.gcd(n_heads, DIFF_HEADS_PER_STEP)
    width = heads * DIFF_V_DIM
    qb, kb, vb = q_col // width, k_col // width, vt_row // width
    q_aug, k_aug = _alibi_lanes(n_heads, seq, rows)
    vec = pl.BlockSpec((1, HEAD_DIM), lambda g, i: (0, 0))
    resident = pl.Buffered(1)
    return pl.pallas_call(
        functools.partial(_diff_kernel, tq=tq, seq=seq, heads=heads, lambda_init=lambda_init),
        grid=(n_heads // heads, seq // tq),
        in_specs=[pl.BlockSpec((tq, width), lambda g, i: (i, qb + g)),
                  pl.BlockSpec((rows, width), lambda g, i: (0, kb + g), pipeline_mode=resident),
                  pl.BlockSpec((rows, AUG), lambda g, i: (0, 0), pipeline_mode=resident),
                  pl.BlockSpec((heads, 1, AUG), lambda g, i: (g, 0, 0)),
                  pl.BlockSpec((width, rows), lambda g, i: (vb + g, 0), pipeline_mode=resident),
                  vec, vec, vec, vec,
                  pl.BlockSpec((1, DIFF_V_DIM), lambda g, i: (0, 0))],
        out_specs=pl.BlockSpec((tq, width), lambda g, i: (i, g)),
        out_shape=jax.ShapeDtypeStruct((seq, n_heads * DIFF_V_DIM), BF16),
        scratch_shapes=[pltpu.VMEM((2 * heads, 1, tq), F32), pltpu.VMEM((2 * heads, 1, tq), F32),
                        pltpu.VMEM((2 * heads, DIFF_V_DIM, tq), F32),
                        pltpu.VMEM((2 * heads, tq, HEAD_DIM + AUG), BF16),
                        pltpu.VMEM((2 * heads, tq, tq), F32), pltpu.VMEM((2 * heads, tq, tq), F32)],
        compiler_params=_params("arbitrary", "arbitrary"),
        name="diff_attention",
    )(qk, qk, k_aug, q_aug, vt, *[v.reshape(1, HEAD_DIM).astype(F32) for v in lam_vecs],
      gain.reshape(1, DIFF_V_DIM).astype(F32))


def _out_proj_kernel(od_ref, of_ref, wd_ref, wf_ref, h_ref, o_ref):
    o_ref[...] = (h_ref[...]
                  + jnp.dot(od_ref[...], wd_ref[...].astype(BF16), preferred_element_type=F32)
                  + jnp.dot(of_ref[...], wf_ref[...].astype(BF16), preferred_element_type=F32))


def _out_proj_residual(o_d, o_f, w_out, h):
    seq, dw = o_d.shape
    fw = o_f.shape[1]
    d = w_out.shape[1]
    tm = _pick(seq, 1024, 16)
    tn = _pick(d, 512, 128)
    assert dw == fw
    return pl.pallas_call(
        _out_proj_kernel,
        grid=(seq // tm, d // tn),
        in_specs=[pl.BlockSpec((tm, dw), lambda i, j: (i, 0)),
                  pl.BlockSpec((tm, fw), lambda i, j: (i, 0)),
                  pl.BlockSpec((dw, tn), lambda i, j: (0, j)),
                  pl.BlockSpec((fw, tn), lambda i, j: (1, j)),
                  pl.BlockSpec((tm, tn), lambda i, j: (i, j))],
        out_specs=pl.BlockSpec((tm, tn), lambda i, j: (i, j)),
        out_shape=jax.ShapeDtypeStruct((seq, d), F32),
        compiler_params=_params("arbitrary", "arbitrary"),
        name="mixer_out_proj",
    )(o_d, o_f, w_out, w_out, h)


def kernel(x, meta_tokens, ffn1_norm, ffn1_w_gate, ffn1_w_up, ffn1_w_down, mix_norm, w_in, b_forget, lambda_q1, lambda_k1, lambda_q2, lambda_k2, subln_gain, w_out, ffn2_norm, ffn2_w_gate, ffn2_w_up, ffn2_w_down, final_norm):
    batch, seq, d = x.shape
    depth = ffn1_norm.shape[0]
    assert batch == 1 and depth == 1 and meta_tokens.shape[0] == N_META
    diff_width = d // 2
    fox_width = d - diff_width
    n_diff = diff_width // DIFF_V_DIM
    n_fox = fox_width // HEAD_DIM
    qkv_cols = 3 * diff_width + 3 * fox_width
    lambda_init = 0.8 - 0.6 * math.exp(-0.3 * 0)

    rows = seq + META_BLOCK
    h = jnp.concatenate(
        [x[0], meta_tokens.astype(x.dtype), jnp.zeros((META_BLOCK - N_META, d), x.dtype)], axis=0)

    h = _swiglu_residual(h, rows, ffn1_norm[0], ffn1_w_gate[0], ffn1_w_up[0], ffn1_w_down[0])

    n = _rmsnorm(h, mix_norm[0], rows, BF16)
    q_scale = HEAD_DIM ** -0.5 * LOG2E
    col_scale = np.concatenate([
        np.full((diff_width,), q_scale, np.float32), np.ones((diff_width,), np.float32),
        np.full((fox_width,), q_scale, np.float32), np.ones((fox_width,), np.float32)])
    w_t = w_in[0].T
    qk = _qk_proj(n, w_t, jnp.asarray(col_scale.reshape(1, -1)), diff_width, fox_width)
    vt = _vt_proj(n, w_t, diff_width, fox_width)
    aug = _forget_bias_lanes(n, w_t, qkv_cols, b_forget[0])
    o_d = _diff_attention(qk, vt, seq, n_diff, 0, diff_width, 0,
                          (lambda_q1[0], lambda_k1[0], lambda_q2[0], lambda_k2[0]),
                          subln_gain[0], lambda_init)
    o_f = _fox_attention(qk, aug, vt, seq, n_fox, 2 * diff_width, 2 * diff_width + fox_width,
                         diff_width)
    h = _out_proj_residual(o_d, o_f, w_out[0], h)

    h = _swiglu_residual(h, seq, ffn2_norm[0], ffn2_w_gate[0], ffn2_w_up[0], ffn2_w_down[0])
    return _rmsnorm(h, final_norm, seq, x.dtype)[None]
```
